```python
import jax, jax.numpy as jnp
from jax import lax
import numpy as np

D_MODEL = 2048
BATCH = 4
SEQ = 4096
DEPTH = 1

HEAD_DIM = 128
N_HEADS = D_MODEL // HEAD_DIM
SB_HEADS = N_HEADS // 2
DSA_HEADS = N_HEADS - SB_HEADS
DSA_KV_HEADS = 2
D_SB = SB_HEADS * HEAD_DIM
D_DSA = DSA_HEADS * HEAD_DIM
D_KV = DSA_KV_HEADS * HEAD_DIM
D_MIX = D_SB + D_DSA
IDX_HEADS = 16
IDX_DIM = 64
TOPK_MAX = 256
ROPE_THETA = 500000.0
ROPE_FRACTION_DIV = 4
Q_BLOCK = 128
EPS = 1e-6
D_IN = 4 * D_SB + 2 * D_DSA + 2 * D_KV + IDX_HEADS * IDX_DIM + IDX_DIM + IDX_HEADS

kernel_name = "hybrid_stickbreak_dsa_block"


def rmsnorm(x, g):
    xf = x.astype(jnp.float32)
    y = xf * lax.rsqrt(jnp.mean(xf * xf, axis=-1, keepdims=True) + EPS)
    return (y * g.astype(jnp.float32)).astype(x.dtype)


def partial_rope(x, pos):
    d_rot = x.shape[-1] // ROPE_FRACTION_DIV
    half = d_rot // 2
    inv_freq = jnp.power(ROPE_THETA, -jnp.arange(half, dtype=jnp.float32) / half)
    ang = pos.astype(jnp.float32)[..., None] * inv_freq
    cos = jnp.cos(ang)[:, :, None, :]
    sin = jnp.sin(ang)[:, :, None, :]
    xr = x[..., :d_rot].astype(jnp.float32)
    x1, x2 = xr[..., :half], xr[..., half:]
    rot = jnp.concatenate([x1 * cos - x2 * sin, x2 * cos + x1 * sin], axis=-1)
    return jnp.concatenate([rot.astype(x.dtype), x[..., d_rot:]], axis=-1)


def stick_breaking_attention(q, k, v):
    B, S, H, D = q.shape
    nb = S // Q_BLOCK
    scale = D ** -0.5
    key_pos = jnp.arange(S)
    qb = q.reshape(B, nb, Q_BLOCK, H, D).transpose(1, 0, 2, 3, 4)

    def block(args):
        qi, bi = args
        z = jnp.einsum('bthd,bshd->bhts', qi, k, preferred_element_type=jnp.float32) * scale
        q_pos = bi * Q_BLOCK + jnp.arange(Q_BLOCK)
        mask = (key_pos[None, :] < q_pos[:, None])[None, None]
        log_beta = jnp.where(mask, jax.nn.log_sigmoid(z), -jnp.inf)
        log_1m_beta = jnp.where(mask, jax.nn.log_sigmoid(-z), 0.0)
        tail = lax.cumsum(log_1m_beta, axis=3, reverse=True) - log_1m_beta
        w = jnp.exp(log_beta + tail)
        return jnp.einsum('bhts,bshd->bthd', w.astype(v.dtype), v)

    out = lax.map(block, (qb, jnp.arange(nb)))
    return out.transpose(1, 0, 2, 3, 4).reshape(B, S, H, D)


def dsa_sparse_attention(q, k, v, q_idx, k_idx, w_idx):
    B, S, H, D = q.shape
    G = k.shape[2]
    rep = H // G
    Hi, Di = q_idx.shape[2], q_idx.shape[3]
    topk = min(TOPK_MAX, S // 4)
    nb = S // Q_BLOCK
    scale = D ** -0.5
    idx_scale = Di ** -0.5
    key_pos = jnp.arange(S)
    qb = q.reshape(B, nb, Q_BLOCK, H, D).transpose(1, 0, 2, 3, 4)
    qib = q_idx.reshape(B, nb, Q_BLOCK, Hi, Di).transpose(1, 0, 2, 3, 4)
    wb = w_idx.reshape(B, nb, Q_BLOCK, Hi).transpose(1, 0, 2, 3)

    def block(args):
        qi, qii, wi, bi = args
        q_pos = bi * Q_BLOCK + jnp.arange(Q_BLOCK)
        causal = key_pos[None, :] <= q_pos[:, None]
        rel = jax.nn.relu(jnp.einsum('btid,bsd->btis', qii, k_idx,
                                     preferred_element_type=jnp.float32) * idx_scale)
        score = jnp.einsum('btis,bti->bts', rel, wi.astype(jnp.float32))
        score = jnp.where(causal[None], score, -jnp.inf)
        _, sel = lax.top_k(score, topk)
        valid = sel <= q_pos[None, :, None]
        k_sel = jax.vmap(lambda kb, ib: kb[ib])(k, sel)
        v_sel = jax.vmap(lambda vb, ib: vb[ib])(v, sel)
        qg = qi.reshape(B, Q_BLOCK, G, rep, D)
        s = jnp.einsum('btgrd,btkgd->btgrk', qg, k_sel,
                       preferred_element_type=jnp.float32) * scale
        s = jnp.where(valid[:, :, None, None, :], s, -jnp.inf)
        p = jax.nn.softmax(s, axis=-1)
        o = jnp.einsum('btgrk,btkgd->btgrd', p.astype(v.dtype), v_sel)
        return o.reshape(B, Q_BLOCK, H, D)

    out = lax.map(block, (qb, qib, wb, jnp.arange(nb)))
    return out.transpose(1, 0, 2, 3, 4).reshape(B, S, H, D)


def head_rmsnorm(o, g):
    B, S, H, D = o.shape
    return rmsnorm(o, g.reshape(H, D)).reshape(B, S, H * D)


def setup_inputs(seed: int = 0) -> dict:
    key = jax.random.key(seed)
    ks = jax.random.split(key, 8)
    x = jax.random.normal(ks[0], (BATCH, SEQ, D_MODEL), jnp.float32)
    positions = jnp.broadcast_to(jnp.arange(SEQ, dtype=jnp.int32)[None], (BATCH, SEQ))
    norm_g = 1.0 + 0.02 * jax.random.normal(ks[1], (DEPTH, D_MODEL), jnp.float32)
    w_in = jax.random.normal(ks[2], (DEPTH, D_MODEL, D_IN), jnp.float32) * D_MODEL ** -0.5
    sb_out_g = 1.0 + 0.02 * jax.random.normal(ks[3], (DEPTH, D_SB), jnp.float32)
    dsa_out_g = 1.0 + 0.02 * jax.random.normal(ks[4], (DEPTH, D_DSA), jnp.float32)
    w_out = jax.random.normal(ks[5], (DEPTH, D_MIX, D_MODEL), jnp.float32) * D_MIX ** -0.5
    final_g = 1.0 + 0.02 * jax.random.normal(ks[6], (D_MODEL,), jnp.float32)
    return {"x": x, "positions": positions, "norm_g": norm_g, "w_in": w_in,
            "sb_out_g": sb_out_g, "dsa_out_g": dsa_out_g, "w_out": w_out,
            "final_g": final_g}


def reference(x, positions, norm_g, w_in, sb_out_g, dsa_out_g, w_out, final_g):
    B, S, _ = x.shape
    sizes = (D_SB, D_SB, D_SB, D_SB, D_DSA, D_KV, D_KV, D_DSA,
             IDX_HEADS * IDX_DIM, IDX_DIM, IDX_HEADS)
    for layer in range(DEPTH):
        h = rmsnorm(x, norm_g[layer])
        proj = jnp.einsum('bsd,de->bse', h, w_in[layer])
        parts = []
        off = 0
        for n in sizes:
            parts.append(proj[..., off:off + n])
            off += n
        (sb_q, sb_k, sb_v, sb_gate, dq, dk, dv, dsa_gate, iq, ik, iw) = parts

        o_sb = stick_breaking_attention(sb_q.reshape(B, S, SB_HEADS, HEAD_DIM),
                                        sb_k.reshape(B, S, SB_HEADS, HEAD_DIM),
                                        sb_v.reshape(B, S, SB_HEADS, HEAD_DIM))
        o_sb = head_rmsnorm(o_sb, sb_out_g[layer]) * jax.nn.silu(sb_gate)

        q_d = partial_rope(dq.reshape(B, S, DSA_HEADS, HEAD_DIM), positions)
        k_d = partial_rope(dk.reshape(B, S, DSA_KV_HEADS, HEAD_DIM), positions)
        v_d = dv.reshape(B, S, DSA_KV_HEADS, HEAD_DIM)
        q_i = partial_rope(iq.reshape(B, S, IDX_HEADS, IDX_DIM), positions)
        k_i = partial_rope(ik.reshape(B, S, 1, IDX_DIM), positions)[:, :, 0, :]
        w_i = iw * (IDX_HEADS ** -0.5)
        o_dsa = dsa_sparse_attention(q_d, k_d, v_d, q_i, k_i, w_i)
        o_dsa = head_rmsnorm(o_dsa, dsa_out_g[layer]) * jax.nn.silu(dsa_gate)

        mixed = jnp.concatenate([o_sb, o_dsa], axis=-1)
        x = x + jnp.einsum('bse,ed->bsd', mixed, w_out[layer]).astype(x.dtype)
    return rmsnorm(x, final_g)
```

```python
import functools

import jax
import jax.numpy as jnp
from jax import lax
from jax.experimental import pallas as pl
from jax.experimental.pallas import tpu as pltpu

F32 = jnp.float32
BF16 = jnp.bfloat16
I32 = jnp.int32

HEAD_DIM = 128
SB_HEADS = 8
DSA_HEADS = 8
DSA_KV_HEADS = 2
IDX_HEADS = 16
IDX_DIM = 64
TOPK_MAX = 256
ROPE_THETA = 500000.0
ROPE_FRACTION_DIV = 4
EPS = 1e-6

LANES = 128
VMEM_LIMIT_BYTES = 56 * 1024 * 1024

INT_MIN = -(2**31)
MASK_BIAS = -1e30
SB_DEAD_LOG = -110.0


def _dot_nt(a, b):
    return lax.dot_general(a, b, (((1,), (1,)), ((), ())), preferred_element_type=F32)


def _dot(a, b):
    return jnp.dot(a, b, preferred_element_type=F32)


def _rope_table_kernel(pos_ref, invf_ref, sign_ref, c128_ref, s128_ref, c64_ref, s64_ref):
    pos = pos_ref[...].astype(F32)
    ang128 = pos * invf_ref[0:1, :]
    ang64 = pos * invf_ref[1:2, :]
    c128_ref[...] = jnp.cos(ang128)
    s128_ref[...] = jnp.sin(ang128) * sign_ref[0:1, :]
    c64_ref[...] = jnp.cos(ang64)
    s64_ref[...] = jnp.sin(ang64) * sign_ref[1:2, :]


def _rope_lane_patterns():
    lane = jnp.arange(LANES)
    half128 = HEAD_DIM // ROPE_FRACTION_DIV // 2
    half64 = IDX_DIM // ROPE_FRACTION_DIV // 2
    f128 = jnp.power(ROPE_THETA, -jnp.arange(half128, dtype=F32) / half128)
    f64 = jnp.power(ROPE_THETA, -jnp.arange(half64, dtype=F32) / half64)
    in128 = lane < 2 * half128
    invf128 = jnp.where(in128, f128[lane % half128], 0.0)
    sign128 = jnp.where(in128, jnp.where(lane < half128, -1.0, 1.0), 0.0)
    l64 = lane % IDX_DIM
    in64 = l64 < 2 * half64
    invf64 = jnp.where(in64, f64[l64 % half64], 0.0)
    sign64 = jnp.where(in64, jnp.where(l64 < half64, -1.0, 1.0), 0.0)
    return (jnp.stack([invf128, invf64]).astype(F32), jnp.stack([sign128, sign64]).astype(F32))


def _rope_tables(positions):
    rows = positions.size
    pos = positions.reshape(rows, 1).astype(I32)
    invf, sign = _rope_lane_patterns()
    tr = 1024
    tab = jax.ShapeDtypeStruct((rows, LANES), F32)
    return pl.pallas_call(
        _rope_table_kernel,
        grid=(rows // tr,),
        in_specs=[pl.BlockSpec((tr, 1), lambda r: (r, 0)),
                  pl.BlockSpec((2, LANES), lambda r: (0, 0)),
                  pl.BlockSpec((2, LANES), lambda r: (0, 0))],
        out_specs=[pl.BlockSpec((tr, LANES), lambda r: (r, 0))] * 4,
        out_shape=[tab] * 4,
        name="rope_tables",
    )(pos, invf, sign)


def _apply_rope(x, cos, sin, kind):
    lane = lax.broadcasted_iota(I32, x.shape, 1)
    if kind == "rope128":
        half = HEAD_DIM // ROPE_FRACTION_DIV // 2
        first = lane < half
    else:
        half = IDX_DIM // ROPE_FRACTION_DIV // 2
        first = (lane % IDX_DIM) < half
    partner = jnp.where(first, pltpu.roll(x, LANES - half, 1), pltpu.roll(x, half, 1))
    return x * cos + partner * sin


def _proj_kernel(*refs, kinds, outs, use_rope):
    if use_rope:
        x_ref, g_ref, w_ref, cs_ref, c128_ref, s128_ref, c64_ref, s64_ref = refs[:8]
        rest = refs[8:]
    else:
        x_ref, g_ref, w_ref, cs_ref = refs[:4]
        rest = refs[4:]
    out_refs, h_scr = rest[:len(outs)], rest[len(outs)]

    @pl.when(pl.program_id(1) == 0)
    def _():
        xf = x_ref[...]
        ms = jnp.mean(xf * xf, axis=-1, keepdims=True)
        h_scr[...] = (xf * lax.rsqrt(ms + EPS) * g_ref[...]).astype(BF16)

    acc = _dot(h_scr[...], w_ref[...]) * cs_ref[...]
    for o_ref, (lo, hi) in zip(out_refs, outs):
        for j in range(lo, hi):
            xj = acc[:, j * LANES:(j + 1) * LANES]
            if kinds[j] == "rope128":
                xj = _apply_rope(xj, c128_ref[...], s128_ref[...], "rope128")
            elif kinds[j] == "rope64":
                xj = _apply_rope(xj, c64_ref[...], s64_ref[...], "rope64")
            o_ref[:, (j - lo) * LANES:(j - lo + 1) * LANES] = xj.astype(o_ref.dtype)


def _proj(x2, g, w, colscale, tables, *, tn, kinds, outs, out_dtypes, name):
    m, d = x2.shape
    n = w.shape[1]
    tm = 1024
    use_rope = tables is not None
    in_specs = [pl.BlockSpec((tm, d), lambda i, j: (i, 0)),
                pl.BlockSpec((1, d), lambda i, j: (0, 0)),
                pl.BlockSpec((d, tn), lambda i, j: (0, j)),
                pl.BlockSpec((1, tn), lambda i, j: (0, j))]
    args = [x2, g, w, colscale]
    if use_rope:
        in_specs += [pl.BlockSpec((tm, LANES), lambda i, j: (i, 0))] * 4
        args += list(tables)
    n_tiles = n // tn
    out_specs, out_shape = [], []
    for (lo, hi), dt in zip(outs, out_dtypes):
        width = (hi - lo) * LANES
        out_specs.append(pl.BlockSpec((tm, width), lambda i, j: (i, j)))
        out_shape.append(jax.ShapeDtypeStruct((m, width * n_tiles), dt))
    return pl.pallas_call(
        functools.partial(_proj_kernel, kinds=kinds, outs=outs, use_rope=use_rope),
        grid=(m // tm, n_tiles),
        in_specs=in_specs,
        out_specs=out_specs,
        out_shape=out_shape,
        scratch_shapes=[pltpu.VMEM((tm, d), BF16)],
        compiler_params=pltpu.CompilerParams(
            dimension_semantics=("arbitrary", "arbitrary"), vmem_limit_bytes=VMEM_LIMIT_BYTES),
        name=name,
    )(*args)


def _norm_gate(o, gain, gate):
    ms = jnp.mean(o * o, axis=-1, keepdims=True)
    y = o * lax.rsqrt(ms + EPS) * gain
    gf = gate.astype(F32)
    return y * (gf * (1.0 / (1.0 + jnp.exp(-gf))))


def _sb_kernel(q_ref, k_ref, v_ref, gate_ref, gain_ref, o_ref, *, tq, hps):
    i = pl.program_id(2)
    row = lax.broadcasted_iota(I32, (tq, tq), 0)
    col = lax.broadcasted_iota(I32, (tq, tq), 1)
    later = jnp.where(row > col, 1.0, 0.0).astype(BF16)
    strict = col < row
    qs = [q_ref[:, h * HEAD_DIM:(h + 1) * HEAD_DIM] for h in range(hps)]

    def block(kb, cs, accs, diag):
        start = pl.multiple_of(kb * tq, tq)
        new_cs, new_accs = [], []
        for h in range(hps):
            kblk = k_ref[pl.ds(start, tq), h * HEAD_DIM:(h + 1) * HEAD_DIM]
            vblk = v_ref[pl.ds(start, tq), h * HEAD_DIM:(h + 1) * HEAD_DIM]
            z = _dot_nt(qs[h], kblk)
            sp = jnp.log1p(jnp.exp(-jnp.abs(z)))
            log_beta = jnp.minimum(z, 0.0) - sp
            log_1m = -jnp.maximum(z, 0.0) - sp
            if diag:
                log_beta = jnp.where(strict, log_beta, -jnp.inf)
                log_1m = jnp.where(strict, log_1m, 0.0)
            hi = log_1m.astype(BF16)
            lo = (log_1m - hi.astype(F32)).astype(BF16)
            tail = _dot(hi, later) + _dot(lo, later)
            w = jnp.exp(log_beta + tail + cs[h])
            new_accs.append(accs[h] + _dot(w.astype(BF16), vblk))
            new_cs.append(cs[h] + jnp.sum(log_1m, axis=1, keepdims=True))
        return new_cs, new_accs

    def dead_bound(cs):
        return functools.reduce(jnp.maximum, [jnp.max(c) for c in cs])

    cs0 = [jnp.zeros((tq, 1), F32) for _ in range(hps)]
    accs0 = [jnp.zeros((tq, HEAD_DIM), F32) for _ in range(hps)]
    cs1, accs1 = block(i, cs0, accs0, True)

    def cond(carry):
        kb, bound, _, _ = carry
        return jnp.logical_and(kb >= 0, bound > SB_DEAD_LOG)

    def body(carry):
        kb, _, cs, accs = carry
        cs, accs = block(kb, list(cs), list(accs), False)
        return kb - 1, dead_bound(cs), tuple(cs), tuple(accs)

    _, _, _, accs = lax.while_loop(cond, body, (i - 1, dead_bound(cs1), tuple(cs1), tuple(accs1)))

    for h in range(hps):
        sl = slice(h * HEAD_DIM, (h + 1) * HEAD_DIM)
        o_ref[:, sl] = _norm_gate(accs[h], gain_ref[:, sl], gate_ref[:, sl]).astype(o_ref.dtype)


def _sb_attention(proj, gain, *, batch, seq, q_col, k_col, v_col, gate_col):
    tq, hps = 256, 2
    w = hps * HEAD_DIM
    qb, kb, vb, gb = (c // w for c in (q_col, k_col, v_col, gate_col))
    return pl.pallas_call(
        functools.partial(_sb_kernel, tq=tq, hps=hps),
        grid=(batch, SB_HEADS // hps, seq // tq),
        in_specs=[pl.BlockSpec((None, tq, w), lambda b, h, i: (b, i, qb + h)),
                  pl.BlockSpec((None, seq, w), lambda b, h, i: (b, 0, kb + h)),
                  pl.BlockSpec((None, seq, w), lambda b, h, i: (b, 0, vb + h)),
                  pl.BlockSpec((None, tq, w), lambda b, h, i: (b, i, gb + h)),
                  pl.BlockSpec((1, w), lambda b, h, i: (0, h))],
        out_specs=pl.BlockSpec((None, tq, w), lambda b, h, i: (b, i, h)),
        out_shape=jax.ShapeDtypeStruct((batch, seq, SB_HEADS * HEAD_DIM), BF16),
        compiler_params=pltpu.CompilerParams(
            dimension_semantics=("arbitrary", "arbitrary", "arbitrary"),
            vmem_limit_bytes=VMEM_LIMIT_BYTES),
        name="sb_attention",
    )(proj, proj, proj, proj, gain)


def _dsa_kernel(iq_ref, ik_ref, iw_ref, q_ref, k_ref, v_ref, gate_ref, gain_ref, o_ref,
                key_scr, wb_scr, m_scr, l_scr, acc_scr, *, tq, ck, topk):
    i = pl.program_id(1)
    n_chunks = (i * tq) // ck + 1
    sub = ck // LANES
    rep = DSA_HEADS // DSA_KV_HEADS

    iw = iw_ref[...]
    for h in range(IDX_HEADS):
        wb_scr[h] = jnp.broadcast_to(iw[:, h:h + 1], (tq, LANES))
    row_pos = i * tq + lax.broadcasted_iota(I32, (tq, LANES), 0)
    lane = lax.broadcasted_iota(I32, (tq, LANES), 1)

    def score_chunk(c, carry):
        start = pl.multiple_of(c * ck, ck)
        k_even = ik_ref[pl.ds(start, ck), 0:LANES]
        k_odd = ik_ref[pl.ds(start, ck), LANES:2 * LANES]
        acc = [jnp.zeros((tq, LANES), F32) for _ in range(sub)]
        for p in range(IDX_HEADS // 2):
            qp = iq_ref[:, p * LANES:(p + 1) * LANES]
            for hh, kk in ((2 * p, k_even), (2 * p + 1, k_odd)):
                rel = _dot_nt(qp, kk)
                wbh = wb_scr[hh]
                for j in range(sub):
                    acc[j] = acc[j] + wbh * jnp.maximum(rel[:, j * LANES:(j + 1) * LANES], 0.0)
        for j in range(sub):
            bits = pltpu.bitcast(acc[j], I32)
            key = bits ^ ((bits >> 31) & jnp.int32(0x7FFFFFFF))
            col_pos = c * ck + j * LANES + lane
            key_scr[c, :, j * LANES:(j + 1) * LANES] = jnp.where(col_pos <= row_pos, key, jnp.int32(INT_MIN))
        return carry

    lax.fori_loop(0, n_chunks, score_chunk, 0)

    def count_ge(cand):
        def body(c, cnt):
            kc = key_scr[c]
            for j in range(sub):
                cnt = cnt + jnp.where(kc[:, j * LANES:(j + 1) * LANES] >= cand, 1.0, 0.0)
            return cnt
        cnt = lax.fori_loop(0, n_chunks, body, jnp.zeros((tq, LANES), F32))
        return jnp.sum(cnt, axis=1, keepdims=True)

    zero = jnp.zeros((tq, LANES), I32)
    thr0 = jnp.where(count_ge(zero) >= topk, zero, jnp.int32(INT_MIN))

    def bit_pass(it, thr):
        cand = thr | jnp.left_shift(jnp.int32(1), 30 - it)
        return jnp.where(count_ge(cand) >= topk, cand, thr)

    thr = lax.fori_loop(0, 31, bit_pass, thr0)
    thr = jnp.maximum(thr, jnp.int32(INT_MIN + 1))

    m_scr[...] = jnp.full(m_scr.shape, MASK_BIAS, F32)
    l_scr[...] = jnp.zeros(l_scr.shape, F32)
    acc_scr[...] = jnp.zeros(acc_scr.shape, F32)

    def attend_chunk(c, carry):
        start = pl.multiple_of(c * ck, ck)
        kc = key_scr[c]
        bias = jnp.concatenate(
            [jnp.where(kc[:, j * LANES:(j + 1) * LANES] >= thr, 0.0, MASK_BIAS) for j in range(sub)], axis=1)
        for g in range(DSA_KV_HEADS):
            kg = k_ref[pl.ds(start, ck), g * HEAD_DIM:(g + 1) * HEAD_DIM]
            vg = v_ref[pl.ds(start, ck), g * HEAD_DIM:(g + 1) * HEAD_DIM]
            for r in range(rep):
                h = g * rep + r
                s = _dot_nt(q_ref[:, h * HEAD_DIM:(h + 1) * HEAD_DIM], kg) + bias
                m_old = m_scr[h]
                m_new = jnp.maximum(m_old, jnp.max(s, axis=1, keepdims=True))
                alpha = jnp.exp(m_old - m_new)
                p = jnp.exp(s - m_new)
                l_scr[h] = alpha * l_scr[h] + jnp.sum(p, axis=1, keepdims=True)
                acc_scr[h] = alpha * acc_scr[h] + _dot(p.astype(BF16), vg)
                m_scr[h] = m_new
        return carry

    lax.fori_loop(0, n_chunks, attend_chunk, 0)

    for h in range(DSA_HEADS):
        sl = slice(h * HEAD_DIM, (h + 1) * HEAD_DIM)
        o = acc_scr[h] / l_scr[h]
        o_ref[:, sl] = _norm_gate(o, gain_ref[:, sl], gate_ref[:, sl]).astype(o_ref.dtype)


def _dsa_attention(iq, ikp, iw, rproj, pproj, gain, *, batch, seq, q_col, k_col, v_col, gate_col):
    tq, ck = 128, 512
    topk = min(TOPK_MAX, seq // 4)
    dq = DSA_HEADS * HEAD_DIM
    dkv = DSA_KV_HEADS * HEAD_DIM
    return pl.pallas_call(
        functools.partial(_dsa_kernel, tq=tq, ck=ck, topk=topk),
        grid=(batch, seq // tq),
        in_specs=[pl.BlockSpec((None, tq, IDX_HEADS * IDX_DIM), lambda b, i: (b, i, 0)),
                  pl.BlockSpec((None, seq, 2 * LANES), lambda b, i: (b, 0, 0)),
                  pl.BlockSpec((None, tq, LANES), lambda b, i: (b, i, 0)),
                  pl.BlockSpec((None, tq, dq), lambda b, i: (b, i, q_col // dq)),
                  pl.BlockSpec((None, seq, dkv), lambda b, i: (b, 0, k_col // dkv)),
                  pl.BlockSpec((None, seq, dkv), lambda b, i: (b, 0, v_col // dkv)),
                  pl.BlockSpec((None, tq, dq), lambda b, i: (b, i, gate_col // dq)),
                  pl.BlockSpec((1, dq), lambda b, i: (0, 0))],
        out_specs=pl.BlockSpec((None, tq, dq), lambda b, i: (b, i, 0)),
        out_shape=jax.ShapeDtypeStruct((batch, seq, dq), BF16),
        scratch_shapes=[pltpu.VMEM((seq // ck, tq, ck), I32),
                        pltpu.VMEM((IDX_HEADS, tq, LANES), F32),
                        pltpu.VMEM((DSA_HEADS, tq, 1), F32),
                        pltpu.VMEM((DSA_HEADS, tq, 1), F32),
                        pltpu.VMEM((DSA_HEADS, tq, HEAD_DIM), F32)],
        compiler_params=pltpu.CompilerParams(
            dimension_semantics=("arbitrary", "arbitrary"), vmem_limit_bytes=VMEM_LIMIT_BYTES),
        name="dsa_attention",
    )(iq, ikp, iw, rproj, rproj, pproj, pproj, gain)


def _out_kernel(a_ref, b_ref, wa_ref, wb_ref, x_ref, g_ref, o_ref):
    y = x_ref[...] + _dot(a_ref[...], wa_ref[...]) + _dot(b_ref[...], wb_ref[...])
    ms = jnp.mean(y * y, axis=-1, keepdims=True)
    o_ref[...] = y * lax.rsqrt(ms + EPS) * g_ref[...]


def _out_proj(mix_a, mix_b, w_a, w_b, x2, final_g):
    m, d = x2.shape
    tm = 512
    ka, kb = mix_a.shape[1], mix_b.shape[1]
    return pl.pallas_call(
        _out_kernel,
        grid=(m // tm,),
        in_specs=[pl.BlockSpec((tm, ka), lambda i: (i, 0)),
                  pl.BlockSpec((tm, kb), lambda i: (i, 0)),
                  pl.BlockSpec((ka, d), lambda i: (0, 0)),
                  pl.BlockSpec((kb, d), lambda i: (0, 0)),
                  pl.BlockSpec((tm, d), lambda i: (i, 0)),
                  pl.BlockSpec((1, d), lambda i: (0, 0))],
        out_specs=pl.BlockSpec((tm, d), lambda i: (i, 0)),
        out_shape=jax.ShapeDtypeStruct((m, d), F32),
        compiler_params=pltpu.CompilerParams(
            dimension_semantics=("arbitrary",), vmem_limit_bytes=VMEM_LIMIT_BYTES),
        name="out_proj",
    )(mix_a, mix_b, w_a, w_b, x2, final_g)


def kernel(x, positions, norm_g, w_in, sb_out_g, dsa_out_g, w_out, final_g):
    batch, seq, d_model = x.shape
    d_sb = SB_HEADS * HEAD_DIM
    d_dsa = DSA_HEADS * HEAD_DIM
    d_kv = DSA_KV_HEADS * HEAD_DIM
    d_iq = IDX_HEADS * IDX_DIM
    depth = norm_g.shape[0]
    assert depth == 1 and d_sb + d_dsa == w_out.shape[1]
    m = batch * seq
    x2 = x.reshape(m, d_model)
    layer = 0

    sizes = (d_sb, d_sb, d_sb, d_sb, d_dsa, d_kv, d_kv, d_dsa, d_iq, IDX_DIM, IDX_HEADS)
    offs = [0]
    for s in sizes:
        offs.append(offs[-1] + s)
    wl = w_in[layer]
    (w_sbq, w_sbk, w_sbv, w_sbg, w_dq, w_dk, w_dv, w_dg, w_iq, w_ik, w_iw) = [
        wl[:, offs[t]:offs[t + 1]] for t in range(len(sizes))]

    tables = _rope_tables(positions)
    g_in = norm_g[layer].reshape(1, d_model)
    att_scale = HEAD_DIM ** -0.5

    w_r = jnp.concatenate([w_dq, w_dk], axis=1).astype(BF16)
    cs_r = jnp.concatenate([jnp.full((d_dsa,), att_scale, F32), jnp.ones((d_kv,), F32)]).reshape(1, -1)
    n_r = w_r.shape[1] // LANES
    (rproj,) = _proj(x2, g_in, w_r, cs_r, tables, tn=w_r.shape[1] // 2,
                     kinds=("rope128",) * (n_r // 2), outs=((0, n_r // 2),), out_dtypes=(BF16,), name="proj_rope")

    w_p = jnp.concatenate([w_dg, w_dv, w_sbq, w_sbk, w_sbv, w_sbg], axis=1).astype(BF16)
    cs_p = jnp.concatenate([jnp.ones((d_dsa + d_kv,), F32), jnp.full((d_sb,), att_scale, F32),
                            jnp.ones((3 * d_sb,), F32)]).reshape(1, -1)
    tn_p = 768
    (pproj,) = _proj(x2, g_in, w_p, cs_p, None, tn=tn_p, kinds=("plain",) * (tn_p // LANES),
                     outs=((0, tn_p // LANES),), out_dtypes=(BF16,), name="proj_plain")

    zeros64 = jnp.zeros((d_model, IDX_DIM), w_in.dtype)
    w_i = jnp.concatenate([w_iq, w_ik, zeros64, zeros64, w_ik, w_iw,
                           jnp.zeros((d_model, LANES - IDX_HEADS), w_in.dtype)], axis=1).astype(BF16)
    n_i = w_i.shape[1] // LANES
    n_iq = d_iq // LANES
    cs_i = jnp.concatenate([jnp.ones((d_iq + 2 * LANES,), F32),
                            jnp.full((LANES,), (IDX_HEADS ** -0.5) * (IDX_DIM ** -0.5), F32)]).reshape(1, -1)
    kinds_i = ("rope64",) * (n_iq + 2) + ("plain",)
    iq, ikp, iw = _proj(x2, g_in, w_i, cs_i, tables, tn=w_i.shape[1], kinds=kinds_i,
                        outs=((0, n_iq), (n_iq, n_iq + 2), (n_iq + 2, n_i)),
                        out_dtypes=(BF16, BF16, F32), name="proj_index")

    rproj3 = rproj.reshape(batch, seq, -1)
    pproj3 = pproj.reshape(batch, seq, -1)
    c_dv = d_dsa
    c_sbq = d_dsa + d_kv
    mix_sb = _sb_attention(pproj3, sb_out_g[layer].reshape(1, d_sb), batch=batch, seq=seq,
                           q_col=c_sbq, k_col=c_sbq + d_sb, v_col=c_sbq + 2 * d_sb, gate_col=c_sbq + 3 * d_sb)
    mix_dsa = _dsa_attention(iq.reshape(batch, seq, -1), ikp.reshape(batch, seq, -1), iw.reshape(batch, seq, -1),
                             rproj3, pproj3, dsa_out_g[layer].reshape(1, d_dsa), batch=batch, seq=seq,
                             q_col=0, k_col=d_dsa, v_col=c_dv, gate_col=0)

    w_o = w_out[layer].astype(BF16)
    out = _out_proj(mix_sb.reshape(m, d_sb), mix_dsa.reshape(m, d_dsa), w_o[:d_sb], w_o[d_sb:], x2,
                    final_g.reshape(1, d_model))
    return out.reshape(batch, seq, d_model)
```

```python
import functools

import jax
import jax.numpy as jnp
from jax import lax
from jax.experimental import pallas as pl
from jax.experimental.pallas import tpu as pltpu

F32 = jnp.float32
BF16 = jnp.bfloat16
I32 = jnp.int32

HEAD_DIM = 128
SB_HEADS = 8
DSA_HEADS = 8
DSA_KV_HEADS = 2
IDX_HEADS = 16
IDX_DIM = 64
TOPK_MAX = 256
ROPE_THETA = 500000.0
ROPE_FRACTION_DIV = 4
EPS = 1e-6

LANES = 128
VMEM_LIMIT_BYTES = 56 * 1024 * 1024

MASK_BIAS = -1e30
SB_DEAD_LOG = -110.0
SEARCH_PASS_CAP = 320
SEARCH_UNROLL = 4


def _dot_nt(a, b):
    return lax.dot_general(a, b, (((1,), (1,)), ((), ())), preferred_element_type=F32)


def _dot(a, b):
    return jnp.dot(a, b, preferred_element_type=F32)


def _rope_table_kernel(pos_ref, invf_ref, sign_ref, c128_ref, s128_ref, c64_ref, s64_ref):
    pos = pos_ref[...].astype(F32)
    ang128 = pos * invf_ref[0:1, :]
    ang64 = pos * invf_ref[1:2, :]
    c128_ref[...] = jnp.cos(ang128)
    s128_ref[...] = jnp.sin(ang128) * sign_ref[0:1, :]
    c64_ref[...] = jnp.cos(ang64)
    s64_ref[...] = jnp.sin(ang64) * sign_ref[1:2, :]


def _rope_lane_patterns():
    lane = jnp.arange(LANES)
    half128 = HEAD_DIM // ROPE_FRACTION_DIV // 2
    half64 = IDX_DIM // ROPE_FRACTION_DIV // 2
    f128 = jnp.power(ROPE_THETA, -jnp.arange(half128, dtype=F32) / half128)
    f64 = jnp.power(ROPE_THETA, -jnp.arange(half64, dtype=F32) / half64)
    in128 = lane < 2 * half128
    invf128 = jnp.where(in128, f128[lane % half128], 0.0)
    sign128 = jnp.where(in128, jnp.where(lane < half128, -1.0, 1.0), 0.0)
    l64 = lane % IDX_DIM
    in64 = l64 < 2 * half64
    invf64 = jnp.where(in64, f64[l64 % half64], 0.0)
    sign64 = jnp.where(in64, jnp.where(l64 < half64, -1.0, 1.0), 0.0)
    return (jnp.stack([invf128, invf64]).astype(F32), jnp.stack([sign128, sign64]).astype(F32))


def _rope_tables(positions):
    rows = positions.size
    pos = positions.reshape(rows, 1).astype(I32)
    invf, sign = _rope_lane_patterns()
    tr = 1024
    tab = jax.ShapeDtypeStruct((rows, LANES), F32)
    return pl.pallas_call(
        _rope_table_kernel,
        grid=(rows // tr,),
        in_specs=[pl.BlockSpec((tr, 1), lambda r: (r, 0)),
                  pl.BlockSpec((2, LANES), lambda r: (0, 0)),
                  pl.BlockSpec((2, LANES), lambda r: (0, 0))],
        out_specs=[pl.BlockSpec((tr, LANES), lambda r: (r, 0))] * 4,
        out_shape=[tab] * 4,
        name="rope_tables",
    )(pos, invf, sign)


def _apply_rope(x, cos, sin, kind):
    lane = lax.broadcasted_iota(I32, x.shape, 1)
    if kind == "rope128":
        half = HEAD_DIM // ROPE_FRACTION_DIV // 2
        first = lane < half
    else:
        half = IDX_DIM // ROPE_FRACTION_DIV // 2
        first = (lane % IDX_DIM) < half
    partner = jnp.where(first, pltpu.roll(x, LANES - half, 1), pltpu.roll(x, half, 1))
    return x * cos + partner * sin


def _proj_kernel(*refs, kinds, outs, use_rope):
    if use_rope:
        x_ref, g_ref, w_ref, cs_ref, c128_ref, s128_ref, c64_ref, s64_ref = refs[:8]
        rest = refs[8:]
    else:
        x_ref, g_ref, w_ref, cs_ref = refs[:4]
        rest = refs[4:]
    out_refs, h_scr = rest[:len(outs)], rest[len(outs)]

    @pl.when(pl.program_id(1) == 0)
    def _():
        xf = x_ref[...]
        ms = jnp.mean(xf * xf, axis=-1, keepdims=True)
        h_scr[...] = (xf * lax.rsqrt(ms + EPS) * g_ref[...]).astype(BF16)

    acc = _dot(h_scr[...], w_ref[...]) * cs_ref[...]
    for o_ref, (lo, hi) in zip(out_refs, outs):
        for j in range(lo, hi):
            xj = acc[:, j * LANES:(j + 1) * LANES]
            if kinds[j] == "rope128":
                xj = _apply_rope(xj, c128_ref[...], s128_ref[...], "rope128")
            elif kinds[j] == "rope64":
                xj = _apply_rope(xj, c64_ref[...], s64_ref[...], "rope64")
            o_ref[:, (j - lo) * LANES:(j - lo + 1) * LANES] = xj.astype(o_ref.dtype)


def _proj(x2, g, w, colscale, tables, *, tn, kinds, outs, out_dtypes, name):
    m, d = x2.shape
    n = w.shape[1]
    tm = 1024
    use_rope = tables is not None
    in_specs = [pl.BlockSpec((tm, d), lambda i, j: (i, 0)),
                pl.BlockSpec((1, d), lambda i, j: (0, 0)),
                pl.BlockSpec((d, tn), lambda i, j: (0, j)),
                pl.BlockSpec((1, tn), lambda i, j: (0, j))]
    args = [x2, g, w, colscale]
    if use_rope:
        in_specs += [pl.BlockSpec((tm, LANES), lambda i, j: (i, 0))] * 4
        args += list(tables)
    n_tiles = n // tn
    out_specs, out_shape = [], []
    for (lo, hi), dt in zip(outs, out_dtypes):
        width = (hi - lo) * LANES
        out_specs.append(pl.BlockSpec((tm, width), lambda i, j: (i, j)))
        out_shape.append(jax.ShapeDtypeStruct((m, width * n_tiles), dt))
    return pl.pallas_call(
        functools.partial(_proj_kernel, kinds=kinds, outs=outs, use_rope=use_rope),
        grid=(m // tm, n_tiles),
        in_specs=in_specs,
        out_specs=out_specs,
        out_shape=out_shape,
        scratch_shapes=[pltpu.VMEM((tm, d), BF16)],
        compiler_params=pltpu.CompilerParams(
            dimension_semantics=("arbitrary", "arbitrary"), vmem_limit_bytes=VMEM_LIMIT_BYTES),
        name=name,
    )(*args)


def _norm_gate(o, gain, gate):
    ms = jnp.mean(o * o, axis=-1, keepdims=True)
    y = o * lax.rsqrt(ms + EPS) * gain
    gf = gate.astype(F32)
    return y * (gf * (1.0 / (1.0 + jnp.exp(-gf))))


def _sb_kernel(q_ref, k_ref, v_ref, gate_ref, gain_ref, o_ref, *, tq, hps):
    i = pl.program_id(2)
    row = lax.broadcasted_iota(I32, (tq, tq), 0)
    col = lax.broadcasted_iota(I32, (tq, tq), 1)
    later = jnp.where(row > col, 1.0, 0.0).astype(BF16)
    strict = col < row
    qs = [q_ref[:, h * HEAD_DIM:(h + 1) * HEAD_DIM] for h in range(hps)]

    def block(kb, cs, accs, diag):
        start = pl.multiple_of(kb * tq, tq)
        new_cs, new_accs = [], []
        for h in range(hps):
            kblk = k_ref[pl.ds(start, tq), h * HEAD_DIM:(h + 1) * HEAD_DIM]
            vblk = v_ref[pl.ds(start, tq), h * HEAD_DIM:(h + 1) * HEAD_DIM]
            z = _dot_nt(qs[h], kblk)
            sp = jnp.log1p(jnp.exp(-jnp.abs(z)))
            log_beta = jnp.minimum(z, 0.0) - sp
            log_1m = -jnp.maximum(z, 0.0) - sp
            if diag:
                log_beta = jnp.where(strict, log_beta, -jnp.inf)
                log_1m = jnp.where(strict, log_1m, 0.0)
            hi = log_1m.astype(BF16)
            lo = (log_1m - hi.astype(F32)).astype(BF16)
            tail = _dot(hi, later) + _dot(lo, later)
            w = jnp.exp(log_beta + tail + cs[h])
            new_accs.append(accs[h] + _dot(w.astype(BF16), vblk))
            new_cs.append(cs[h] + jnp.sum(log_1m, axis=1, keepdims=True))
        return new_cs, new_accs

    def dead_bound(cs):
        return functools.reduce(jnp.maximum, [jnp.max(c) for c in cs])

    cs0 = [jnp.zeros((tq, 1), F32) for _ in range(hps)]
    accs0 = [jnp.zeros((tq, HEAD_DIM), F32) for _ in range(hps)]
    cs1, accs1 = block(i, cs0, accs0, True)

    def cond(carry):
        kb, bound, _, _ = carry
        return jnp.logical_and(kb >= 0, bound > SB_DEAD_LOG)

    def body(carry):
        kb, _, cs, accs = carry
        cs, accs = block(kb, list(cs), list(accs), False)
        return kb - 1, dead_bound(cs), tuple(cs), tuple(accs)

    _, _, _, accs = lax.while_loop(cond, body, (i - 1, dead_bound(cs1), tuple(cs1), tuple(accs1)))

    for h in range(hps):
        sl = slice(h * HEAD_DIM, (h + 1) * HEAD_DIM)
        o_ref[:, sl] = _norm_gate(accs[h], gain_ref[:, sl], gate_ref[:, sl]).astype(o_ref.dtype)


def _sb_attention(proj, gain, *, batch, seq, q_col, k_col, v_col, gate_col):
    tq, hps = 256, 2
    w = hps * HEAD_DIM
    qb, kb, vb, gb = (c // w for c in (q_col, k_col, v_col, gate_col))
    return pl.pallas_call(
        functools.partial(_sb_kernel, tq=tq, hps=hps),
        grid=(batch, SB_HEADS // hps, seq // tq),
        in_specs=[pl.BlockSpec((None, tq, w), lambda b, h, i: (b, i, qb + h)),
                  pl.BlockSpec((None, seq, w), lambda b, h, i: (b, 0, kb + h)),
                  pl.BlockSpec((None, seq, w), lambda b, h, i: (b, 0, vb + h)),
                  pl.BlockSpec((None, tq, w), lambda b, h, i: (b, i, gb + h)),
                  pl.BlockSpec((1, w), lambda b, h, i: (0, h))],
        out_specs=pl.BlockSpec((None, tq, w), lambda b, h, i: (b, i, h)),
        out_shape=jax.ShapeDtypeStruct((batch, seq, SB_HEADS * HEAD_DIM), BF16),
        compiler_params=pltpu.CompilerParams(
            dimension_semantics=("arbitrary", "arbitrary", "arbitrary"),
            vmem_limit_bytes=VMEM_LIMIT_BYTES),
        name="sb_attention",
    )(proj, proj, proj, proj, gain)


def _dsa_kernel(iq_ref, ik_ref, iw_ref, q_ref, k_ref, v_ref, gate_ref, gain_ref, o_ref,
                sc_scr, wb_scr, qs_scr, m_scr, l_scr, acc_scr, *, tq, ck, topk, seq):
    i = pl.program_id(1)
    n_chunks = (i * tq) // ck + 1
    sub = ck // LANES
    rep = DSA_HEADS // DSA_KV_HEADS
    kf = float(topk)

    iw = iw_ref[...]
    for h in range(IDX_HEADS):
        wb_scr[h] = jnp.broadcast_to(iw[:, h:h + 1], (tq, LANES))
    row_pos = i * tq + lax.broadcasted_iota(I32, (tq, LANES), 0)
    lane = lax.broadcasted_iota(I32, (tq, LANES), 1)

    def score_chunk(c, carry):
        mn, mx = carry
        start = pl.multiple_of(c * ck, ck)
        k_even = ik_ref[pl.ds(start, ck), 0:LANES]
        k_odd = ik_ref[pl.ds(start, ck), LANES:2 * LANES]
        acc = [jnp.zeros((tq, LANES), F32) for _ in range(sub)]
        for p in range(IDX_HEADS // 2):
            qp = iq_ref[:, p * LANES:(p + 1) * LANES]
            for hh, kk in ((2 * p, k_even), (2 * p + 1, k_odd)):
                rel = _dot_nt(qp, kk)
                wbh = wb_scr[hh]
                for j in range(sub):
                    acc[j] = acc[j] + wbh * jnp.maximum(rel[:, j * LANES:(j + 1) * LANES], 0.0)
        for j in range(sub):
            causal = (c * ck + j * LANES + lane) <= row_pos
            sc_scr[c, :, j * LANES:(j + 1) * LANES] = jnp.where(causal, acc[j], -jnp.inf)
            mx = jnp.maximum(mx, jnp.where(causal, acc[j], -jnp.inf))
            mn = jnp.minimum(mn, jnp.where(causal, acc[j], jnp.inf))
        return mn, mx

    mn, mx = lax.fori_loop(0, n_chunks, score_chunk,
                           (jnp.full((tq, LANES), jnp.inf, F32), jnp.full((tq, LANES), -jnp.inf, F32)))
    row_min = jnp.min(mn, axis=1, keepdims=True)
    row_max = jnp.max(mx, axis=1, keepdims=True)

    def count_where(pred):
        def body(c, cnt):
            sc = sc_scr[c]
            for j in range(sub):
                cnt = cnt + jnp.where(pred(sc[:, j * LANES:(j + 1) * LANES], c, j), 1.0, 0.0)
            return cnt
        cnt = lax.fori_loop(0, n_chunks, body, jnp.zeros((tq, LANES), F32))
        return jnp.sum(cnt, axis=1, keepdims=True)

    n_valid = (i * tq + lax.broadcasted_iota(I32, (tq, 1), 0) + 1).astype(F32)
    few = n_valid <= kf

    def bisect_pass(state):
        lo, hi, cnt_lo, done = state
        mid = 0.5 * lo + 0.5 * hi
        no_gap = jnp.logical_or(mid <= lo, mid >= hi)
        midb = jnp.broadcast_to(mid, (tq, LANES))
        cnt = count_where(lambda s, c, j: s >= midb)
        ge = cnt >= kf
        live = done < 0.5
        up = jnp.logical_and(live, ge)
        down = jnp.logical_and(live, jnp.logical_not(ge))
        lo = jnp.where(up, mid, lo)
        cnt_lo = jnp.where(up, cnt, cnt_lo)
        hi = jnp.where(down, mid, hi)
        done = jnp.where(jnp.logical_or(cnt_lo == kf, no_gap), 1.0, done)
        return lo, hi, cnt_lo, done

    def search_cond(carry):
        it, all_done = carry[0], carry[1]
        return jnp.logical_and(it < SEARCH_PASS_CAP, all_done < 0.5)

    def search_body(carry):
        state = carry[2:]
        for _ in range(SEARCH_UNROLL):
            state = bisect_pass(state)
        return (carry[0] + SEARCH_UNROLL, jnp.min(state[3])) + tuple(state)

    hi0 = row_max + (row_max - row_min) * (2.0 ** -10) + 1e-30
    done0 = jnp.where(few, 1.0, 0.0)
    carry = lax.while_loop(search_cond, search_body,
                           (jnp.int32(0), jnp.min(done0), row_min, hi0, n_valid, done0))
    thr, cnt_thr = carry[2], carry[4]
    thr_b = jnp.broadcast_to(thr, (tq, LANES))

    excess = jnp.logical_and(cnt_thr > kf, jnp.logical_not(few))

    @pl.when(jnp.max(jnp.where(excess, 1.0, 0.0)) > 0.5)
    def _():
        need = kf - count_where(lambda s, c, j: s > thr_b)

        def col_f(c, j):
            return (c * ck + j * LANES + lane).astype(F32)

        def pos_pass(b, last_short):
            cand = last_short + jnp.left_shift(jnp.int32(1), b).astype(F32)
            cand_b = jnp.broadcast_to(cand, (tq, LANES))
            cnt = count_where(lambda s, c, j: jnp.logical_and(s == thr_b, col_f(c, j) <= cand_b))
            return jnp.where(cnt < need, cand, last_short)

        n_bits = (seq - 1).bit_length()
        last_short = lax.fori_loop(0, n_bits, lambda t, x: pos_pass(n_bits - 1 - t, x),
                                   jnp.full((tq, 1), -1.0, F32))
        keep_upto = jnp.broadcast_to(jnp.where(excess, last_short + 1.0, float(seq)), (tq, LANES))

        def drop(c, carry):
            sc = sc_scr[c]
            for j in range(sub):
                s = sc[:, j * LANES:(j + 1) * LANES]
                cut = jnp.logical_and(s == thr_b, col_f(c, j) > keep_upto)
                sc_scr[c, :, j * LANES:(j + 1) * LANES] = jnp.where(cut, -jnp.inf, s)
            return carry

        lax.fori_loop(0, n_chunks, drop, 0)

    for g in range(DSA_KV_HEADS):
        for r in range(rep):
            h = g * rep + r
            qs_scr[g, r * tq:(r + 1) * tq, :] = q_ref[:, h * HEAD_DIM:(h + 1) * HEAD_DIM]
    m_scr[...] = jnp.full(m_scr.shape, MASK_BIAS, F32)
    l_scr[...] = jnp.zeros(l_scr.shape, F32)
    acc_scr[...] = jnp.zeros(acc_scr.shape, F32)

    def attend_chunk(c, carry):
        start = pl.multiple_of(c * ck, ck)
        sc = sc_scr[c]
        bias = jnp.concatenate(
            [jnp.where(sc[:, j * LANES:(j + 1) * LANES] >= thr_b, 0.0, MASK_BIAS) for j in range(sub)], axis=1)
        bias = jnp.concatenate([bias] * rep, axis=0)
        out = []
        for g in range(DSA_KV_HEADS):
            m_old, l_old, acc_old = carry[g]
            kg = k_ref[pl.ds(start, ck), g * HEAD_DIM:(g + 1) * HEAD_DIM]
            vg = v_ref[pl.ds(start, ck), g * HEAD_DIM:(g + 1) * HEAD_DIM]
            s = _dot_nt(qs_scr[g], kg) + bias
            m_new = jnp.maximum(m_old, jnp.max(s, axis=1, keepdims=True))
            alpha = jnp.exp(m_old - m_new)
            p = jnp.exp(s - m_new)
            out.append((m_new, alpha * l_old + jnp.sum(p, axis=1, keepdims=True),
                        alpha * acc_old + _dot(p.astype(BF16), vg)))
        return tuple(out)

    init = tuple((jnp.full((rep * tq, 1), MASK_BIAS, F32), jnp.zeros((rep * tq, 1), F32),
                  jnp.zeros((rep * tq, HEAD_DIM), F32)) for _ in range(DSA_KV_HEADS))
    fin = lax.fori_loop(0, n_chunks, attend_chunk, init)
    for g in range(DSA_KV_HEADS):
        acc_scr[g] = fin[g][2]
        l_scr[g] = fin[g][1]

    for g in range(DSA_KV_HEADS):
        o = acc_scr[g] / l_scr[g]
        for r in range(rep):
            sl = slice((g * rep + r) * HEAD_DIM, (g * rep + r + 1) * HEAD_DIM)
            o_ref[:, sl] = _norm_gate(o[r * tq:(r + 1) * tq], gain_ref[:, sl], gate_ref[:, sl]).astype(o_ref.dtype)


def _dsa_attention(iq, ikp, iw, rproj, pproj, gain, *, batch, seq, q_col, k_col, v_col, gate_col):
    tq, ck = 128, 512
    topk = min(TOPK_MAX, seq // 4)
    dq = DSA_HEADS * HEAD_DIM
    dkv = DSA_KV_HEADS * HEAD_DIM
    rep = DSA_HEADS // DSA_KV_HEADS
    return pl.pallas_call(
        functools.partial(_dsa_kernel, tq=tq, ck=ck, topk=topk, seq=seq),
        grid=(batch, seq // tq),
        in_specs=[pl.BlockSpec((None, tq, IDX_HEADS * IDX_DIM), lambda b, i: (b, i, 0)),
                  pl.BlockSpec((None, seq, 2 * LANES), lambda b, i: (b, 0, 0)),
                  pl.BlockSpec((None, tq, LANES), lambda b, i: (b, i, 0)),
                  pl.BlockSpec((None, tq, dq), lambda b, i: (b, i, q_col // dq)),
                  pl.BlockSpec((None, seq, dkv), lambda b, i: (b, 0, k_col // dkv)),
                  pl.BlockSpec((None, seq, dkv), lambda b, i: (b, 0, v_col // dkv)),
                  pl.BlockSpec((None, tq, dq), lambda b, i: (b, i, gate_col // dq)),
                  pl.BlockSpec((1, dq), lambda b, i: (0, 0))],
        out_specs=pl.BlockSpec((None, tq, dq), lambda b, i: (b, i, 0)),
        out_shape=jax.ShapeDtypeStruct((batch, seq, dq), BF16),
        scratch_shapes=[pltpu.VMEM((seq // ck, tq, ck), F32),
                        pltpu.VMEM((IDX_HEADS, tq, LANES), F32),
                        pltpu.VMEM((DSA_KV_HEADS, rep * tq, HEAD_DIM), BF16),
                        pltpu.VMEM((DSA_KV_HEADS, rep * tq, 1), F32),
                        pltpu.VMEM((DSA_KV_HEADS, rep * tq, 1), F32),
                        pltpu.VMEM((DSA_KV_HEADS, rep * tq, HEAD_DIM), F32)],
        compiler_params=pltpu.CompilerParams(
            dimension_semantics=("arbitrary", "arbitrary"), vmem_limit_bytes=VMEM_LIMIT_BYTES),
        name="dsa_attention",
    )(iq, ikp, iw, rproj, rproj, pproj, pproj, gain)


def _out_kernel(a_ref, b_ref, wa_ref, wb_ref, x_ref, g_ref, o_ref):
    y = x_ref[...] + _dot(a_ref[...], wa_ref[...]) + _dot(b_ref[...], wb_ref[...])
    ms = jnp.mean(y * y, axis=-1, keepdims=True)
    o_ref[...] = y * lax.rsqrt(ms + EPS) * g_ref[...]


def _out_proj(mix_a, mix_b, w_a, w_b, x2, final_g):
    m, d = x2.shape
    tm = 512
    ka, kb = mix_a.shape[1], mix_b.shape[1]
    return pl.pallas_call(
        _out_kernel,
        grid=(m // tm,),
        in_specs=[pl.BlockSpec((tm, ka), lambda i: (i, 0)),
                  pl.BlockSpec((tm, kb), lambda i: (i, 0)),
                  pl.BlockSpec((ka, d), lambda i: (0, 0)),
                  pl.BlockSpec((kb, d), lambda i: (0, 0)),
                  pl.BlockSpec((tm, d), lambda i: (i, 0)),
                  pl.BlockSpec((1, d), lambda i: (0, 0))],
        out_specs=pl.BlockSpec((tm, d), lambda i: (i, 0)),
        out_shape=jax.ShapeDtypeStruct((m, d), F32),
        compiler_params=pltpu.CompilerParams(
            dimension_semantics=("arbitrary",), vmem_limit_bytes=VMEM_LIMIT_BYTES),
        name="out_proj",
    )(mix_a, mix_b, w_a, w_b, x2, final_g)


def kernel(x, positions, norm_g, w_in, sb_out_g, dsa_out_g, w_out, final_g):
    batch, seq, d_model = x.shape
    d_sb = SB_HEADS * HEAD_DIM
    d_dsa = DSA_HEADS * HEAD_DIM
    d_kv = DSA_KV_HEADS * HEAD_DIM
    d_iq = IDX_HEADS * IDX_DIM
    depth = norm_g.shape[0]
    assert depth == 1 and d_sb + d_dsa == w_out.shape[1]
    m = batch * seq
    x2 = x.reshape(m, d_model)
    layer = 0

    sizes = (d_sb, d_sb, d_sb, d_sb, d_dsa, d_kv, d_kv, d_dsa, d_iq, IDX_DIM, IDX_HEADS)
    offs = [0]
    for s in sizes:
        offs.append(offs[-1] + s)
    wl = w_in[layer]
    (w_sbq, w_sbk, w_sbv, w_sbg, w_dq, w_dk, w_dv, w_dg, w_iq, w_ik, w_iw) = [
        wl[:, offs[t]:offs[t + 1]] for t in range(len(sizes))]

    tables = _rope_tables(positions)
    g_in = norm_g[layer].reshape(1, d_model)
    att_scale = HEAD_DIM ** -0.5

    w_r = jnp.concatenate([w_dq, w_dk], axis=1).astype(BF16)
    cs_r = jnp.concatenate([jnp.full((d_dsa,), att_scale, F32), jnp.ones((d_kv,), F32)]).reshape(1, -1)
    n_r = w_r.shape[1] // LANES
    (rproj,) = _proj(x2, g_in, w_r, cs_r, tables, tn=w_r.shape[1] // 2,
                     kinds=("rope128",) * (n_r // 2), outs=((0, n_r // 2),), out_dtypes=(BF16,), name="proj_rope")

    w_p = jnp.concatenate([w_dg, w_dv, w_sbq, w_sbk, w_sbv, w_sbg], axis=1).astype(BF16)
    cs_p = jnp.concatenate([jnp.ones((d_dsa + d_kv,), F32), jnp.full((d_sb,), att_scale, F32),
                            jnp.ones((3 * d_sb,), F32)]).reshape(1, -1)
    tn_p = 768
    (pproj,) = _proj(x2, g_in, w_p, cs_p, None, tn=tn_p, kinds=("plain",) * (tn_p // LANES),
                     outs=((0, tn_p // LANES),), out_dtypes=(BF16,), name="proj_plain")

    zeros64 = jnp.zeros((d_model, IDX_DIM), w_in.dtype)
    w_i = jnp.concatenate([w_iq, w_ik, zeros64, zeros64, w_ik, w_iw,
                           jnp.zeros((d_model, LANES - IDX_HEADS), w_in.dtype)], axis=1).astype(BF16)
    n_i = w_i.shape[1] // LANES
    n_iq = d_iq // LANES
    cs_i = jnp.concatenate([jnp.ones((d_iq + 2 * LANES,), F32),
                            jnp.full((LANES,), (IDX_HEADS ** -0.5) * (IDX_DIM ** -0.5), F32)]).reshape(1, -1)
    kinds_i = ("rope64",) * (n_iq + 2) + ("plain",)
    iq, ikp, iw = _proj(x2, g_in, w_i, cs_i, tables, tn=w_i.shape[1], kinds=kinds_i,
                        outs=((0, n_iq), (n_iq, n_iq + 2), (n_iq + 2, n_i)),
                        out_dtypes=(BF16, BF16, F32), name="proj_index")

    rproj3 = rproj.reshape(batch, seq, -1)
    pproj3 = pproj.reshape(batch, seq, -1)
    c_dv = d_dsa
    c_sbq = d_dsa + d_kv
    mix_sb = _sb_attention(pproj3, sb_out_g[layer].reshape(1, d_sb), batch=batch, seq=seq,
                           q_col=c_sbq, k_col=c_sbq + d_sb, v_col=c_sbq + 2 * d_sb, gate_col=c_sbq + 3 * d_sb)
    mix_dsa = _dsa_attention(iq.reshape(batch, seq, -1), ikp.reshape(batch, seq, -1), iw.reshape(batch, seq, -1),
                             rproj3, pproj3, dsa_out_g[layer].reshape(1, d_dsa), batch=batch, seq=seq,
                             q_col=0, k_col=d_dsa, v_col=c_dv, gate_col=0)

    w_o = w_out[layer].astype(BF16)
    out = _out_proj(mix_sb.reshape(m, d_sb), mix_dsa.reshape(m, d_dsa), w_o[:d_sb], w_o[d_sb:], x2,
                    final_g.reshape(1, d_model))
    return out.reshape(batch, seq, d_model)
```

```python
import functools

import jax
import jax.numpy as jnp
from jax import lax
from jax.experimental import pallas as pl
from jax.experimental.pallas import tpu as pltpu

F32 = jnp.float32
BF16 = jnp.bfloat16
I32 = jnp.int32

HEAD_DIM = 128
SB_HEADS = 8
DSA_HEADS = 8
DSA_KV_HEADS = 2
IDX_HEADS = 16
IDX_DIM = 64
TOPK_MAX = 256
ROPE_THETA = 500000.0
ROPE_FRACTION_DIV = 4
EPS = 1e-6

LANES = 128
SUBLANES = 8
VMEM_LIMIT_BYTES = 56 * 1024 * 1024

PROJ_ROWS = 1024
OUT_ROWS = 512
SB_BLOCK = 256
SB_HEADS_PER_STEP = 2
DSA_QUERIES = 256
DSA_KEYS = 512
VT_ROWS = HEAD_DIM + 16

MASK_BIAS = -1e30
LOG2_E = 1.4426950408889634
SB_DEAD_LOG = -110.0
SEARCH_PASS_CAP = 320
SEARCH_UNROLL = 4
COUNT_LANES = 8


def _dot_nt(a, b):
    return lax.dot_general(a, b, (((1,), (1,)), ((), ())), preferred_element_type=F32)


def _dot(a, b):
    return jnp.dot(a, b, preferred_element_type=F32)


def _rope_table_kernel(pos_ref, invf_ref, sign_ref, c128_ref, s128_ref, c64_ref, s64_ref):
    pos = pos_ref[...].astype(F32)
    ang128 = pos * invf_ref[0:1, :]
    ang64 = pos * invf_ref[1:2, :]
    c128_ref[...] = jnp.cos(ang128)
    s128_ref[...] = jnp.sin(ang128) * sign_ref[0:1, :]
    c64_ref[...] = jnp.cos(ang64)
    s64_ref[...] = jnp.sin(ang64) * sign_ref[1:2, :]


def _rope_lane_patterns():
    lane = jnp.arange(LANES)
    half128 = HEAD_DIM // ROPE_FRACTION_DIV // 2
    half64 = IDX_DIM // ROPE_FRACTION_DIV // 2
    f128 = jnp.power(ROPE_THETA, -jnp.arange(half128, dtype=F32) / half128)
    f64 = jnp.power(ROPE_THETA, -jnp.arange(half64, dtype=F32) / half64)
    in128 = lane < 2 * half128
    invf128 = jnp.where(in128, f128[lane % half128], 0.0)
    sign128 = jnp.where(in128, jnp.where(lane < half128, -1.0, 1.0), 0.0)
    l64 = lane % IDX_DIM
    in64 = l64 < 2 * half64
    invf64 = jnp.where(in64, f64[l64 % half64], 0.0)
    sign64 = jnp.where(in64, jnp.where(l64 < half64, -1.0, 1.0), 0.0)
    return (jnp.stack([invf128, invf64]).astype(F32), jnp.stack([sign128, sign64]).astype(F32))


def _rope_tables(positions):
    rows = positions.size
    pos = positions.reshape(rows, 1).astype(I32)
    invf, sign = _rope_lane_patterns()
    tr = 1024
    tab = jax.ShapeDtypeStruct((rows, LANES), F32)
    return pl.pallas_call(
        _rope_table_kernel,
        grid=(rows // tr,),
        in_specs=[pl.BlockSpec((tr, 1), lambda r: (r, 0)),
                  pl.BlockSpec((2, LANES), lambda r: (0, 0)),
                  pl.BlockSpec((2, LANES), lambda r: (0, 0))],
        out_specs=[pl.BlockSpec((tr, LANES), lambda r: (r, 0))] * 4,
        out_shape=[tab] * 4,
        name="rope_tables",
    )(pos, invf, sign)


def _apply_rope(x, cos, sin, kind):
    lane = lax.broadcasted_iota(I32, x.shape, 1)
    if kind == "rope128":
        half = HEAD_DIM // ROPE_FRACTION_DIV // 2
        first = lane < half
    else:
        half = IDX_DIM // ROPE_FRACTION_DIV // 2
        first = (lane % IDX_DIM) < half
    partner = jnp.where(first, pltpu.roll(x, LANES - half, 1), pltpu.roll(x, half, 1))
    return x * cos + partner * sin


def _proj_kernel(*refs, kinds, outs, use_rope, use_wt):
    x_ref, g_ref, w_ref, cs_ref = refs[:4]
    rest = refs[4:]
    if use_rope:
        c128_ref, s128_ref, c64_ref, s64_ref = rest[:4]
        rest = rest[4:]
    if use_wt:
        wt_ref, rest = rest[0], rest[1:]
    out_refs, rest = rest[:len(outs)], rest[len(outs):]
    if use_wt:
        t_ref, rest = rest[0], rest[1:]
    h_scr = rest[0]

    @pl.when(pl.program_id(1) == 0)
    def _():
        xf = x_ref[...]
        ms = jnp.mean(xf * xf, axis=-1, keepdims=True)
        h_scr[...] = (xf * lax.rsqrt(ms + EPS) * g_ref[...]).astype(BF16)
        if use_wt:
            t = _dot_nt(wt_ref[...], h_scr[...])
            cw = t_ref.shape[2]
            for c in range(t_ref.shape[0]):
                for g in range(t_ref.shape[1] // VT_ROWS):
                    t_ref[c, g * VT_ROWS:g * VT_ROWS + HEAD_DIM, :] = (
                        t[g * HEAD_DIM:(g + 1) * HEAD_DIM, c * cw:(c + 1) * cw].astype(t_ref.dtype))
                    t_ref[c, g * VT_ROWS + HEAD_DIM:(g + 1) * VT_ROWS, :] = jnp.ones(
                        (VT_ROWS - HEAD_DIM, cw), t_ref.dtype)

    acc = _dot(h_scr[...], w_ref[...]) * cs_ref[...]
    for o_ref, (lo, hi) in zip(out_refs, outs):
        for j in range(lo, hi):
            xj = acc[:, j * LANES:(j + 1) * LANES]
            if kinds[j] == "rope128":
                xj = _apply_rope(xj, c128_ref[...], s128_ref[...], "rope128")
            elif kinds[j] == "rope64":
                xj = _apply_rope(xj, c64_ref[...], s64_ref[...], "rope64")
            o_ref[:, (j - lo) * LANES:(j - lo + 1) * LANES] = xj.astype(o_ref.dtype)


def _proj(x2, g, w, colscale, tables, *, tn, kinds, outs, out_dtypes, name, wt=None, t_chunk=None):
    m, d = x2.shape
    n = w.shape[1]
    tm = PROJ_ROWS
    use_rope = tables is not None
    use_wt = wt is not None
    in_specs = [pl.BlockSpec((tm, d), lambda i, j: (i, 0)),
                pl.BlockSpec((1, d), lambda i, j: (0, 0)),
                pl.BlockSpec((d, tn), lambda i, j: (0, j)),
                pl.BlockSpec((1, tn), lambda i, j: (0, j))]
    args = [x2, g, w, colscale]
    if use_rope:
        in_specs += [pl.BlockSpec((tm, LANES), lambda i, j: (i, 0))] * 4
        args += list(tables)
    if use_wt:
        in_specs.append(pl.BlockSpec(wt.shape, lambda i, j: (0, 0)))
        args.append(wt)
    n_tiles = n // tn
    out_specs, out_shape = [], []
    for (lo, hi), dt in zip(outs, out_dtypes):
        width = (hi - lo) * LANES
        out_specs.append(pl.BlockSpec((tm, width), lambda i, j: (i, j)))
        out_shape.append(jax.ShapeDtypeStruct((m, width * n_tiles), dt))
    if use_wt:
        t_rows = wt.shape[0] // HEAD_DIM * VT_ROWS
        out_specs.append(pl.BlockSpec((tm // t_chunk, t_rows, t_chunk), lambda i, j: (i, 0, 0)))
        out_shape.append(jax.ShapeDtypeStruct((m // t_chunk, t_rows, t_chunk), BF16))
    return pl.pallas_call(
        functools.partial(_proj_kernel, kinds=kinds, outs=outs, use_rope=use_rope, use_wt=use_wt),
        grid=(m // tm, n_tiles),
        in_specs=in_specs,
        out_specs=out_specs,
        out_shape=out_shape,
        scratch_shapes=[pltpu.VMEM((tm, d), BF16)],
        compiler_params=pltpu.CompilerParams(
            dimension_semantics=("arbitrary", "arbitrary"), vmem_limit_bytes=VMEM_LIMIT_BYTES),
        name=name,
    )(*args)


def _norm_gate(o, gain, gate):
    ms = jnp.mean(o * o, axis=-1, keepdims=True)
    y = o * lax.rsqrt(ms + EPS) * gain
    gf = gate.astype(F32)
    return y * (gf * (1.0 / (1.0 + jnp.exp(-gf))))


def _sb_kernel(q_ref, k_ref, v_ref, gate_ref, gain_ref, o_ref, *, tq, hps):
    i = pl.program_id(2)
    row = lax.broadcasted_iota(I32, (tq, tq), 0)
    col = lax.broadcasted_iota(I32, (tq, tq), 1)
    later = jnp.where(row > col, 1.0, 0.0).astype(BF16)
    strict = col < row
    qs = [q_ref[:, h * HEAD_DIM:(h + 1) * HEAD_DIM] for h in range(hps)]

    def block(kb, cs, accs, diag):
        start = pl.multiple_of(kb * tq, tq)
        new_cs, new_accs = [], []
        for h in range(hps):
            kblk = k_ref[pl.ds(start, tq), h * HEAD_DIM:(h + 1) * HEAD_DIM]
            vblk = v_ref[pl.ds(start, tq), h * HEAD_DIM:(h + 1) * HEAD_DIM]
            z = _dot_nt(qs[h], kblk)
            sp = jnp.log(1.0 + jnp.exp(-jnp.abs(z)))
            log_beta = jnp.minimum(z, 0.0) - sp
            log_1m = log_beta - z
            if diag:
                log_beta = jnp.where(strict, log_beta, -jnp.inf)
                log_1m = jnp.where(strict, log_1m, 0.0)
            hi = log_1m.astype(BF16)
            lo = (log_1m - hi.astype(F32)).astype(BF16)
            tail = _dot(hi, later) + _dot(lo, later)
            w = jnp.exp(log_beta + tail + cs[h])
            new_accs.append(accs[h] + _dot(w.astype(BF16), vblk))
            new_cs.append(cs[h] + jnp.sum(log_1m, axis=1, keepdims=True))
        return new_cs, new_accs

    def dead_bound(cs):
        return functools.reduce(jnp.maximum, [jnp.max(c) for c in cs])

    cs0 = [jnp.zeros((tq, 1), F32) for _ in range(hps)]
    accs0 = [jnp.zeros((tq, HEAD_DIM), F32) for _ in range(hps)]
    cs1, accs1 = block(i, cs0, accs0, True)

    def cond(carry):
        kb, bound, _, _ = carry
        return jnp.logical_and(kb >= 0, bound > SB_DEAD_LOG)

    def body(carry):
        kb, _, cs, accs = carry
        cs, accs = block(kb, list(cs), list(accs), False)
        return kb - 1, dead_bound(cs), tuple(cs), tuple(accs)

    _, _, _, accs = lax.while_loop(cond, body, (i - 1, dead_bound(cs1), tuple(cs1), tuple(accs1)))

    for h in range(hps):
        sl = slice(h * HEAD_DIM, (h + 1) * HEAD_DIM)
        o_ref[:, sl] = _norm_gate(accs[h], gain_ref[:, sl], gate_ref[:, sl]).astype(o_ref.dtype)


def _sb_attention(proj, gain, *, batch, seq, q_col, k_col, v_col, gate_col):
    tq, hps = SB_BLOCK, SB_HEADS_PER_STEP
    w = hps * HEAD_DIM
    qb, kb, vb, gb = (c // w for c in (q_col, k_col, v_col, gate_col))
    return pl.pallas_call(
        functools.partial(_sb_kernel, tq=tq, hps=hps),
        grid=(batch, SB_HEADS // hps, seq // tq),
        in_specs=[pl.BlockSpec((None, tq, w), lambda b, h, i: (b, i, qb + h)),
                  pl.BlockSpec((None, seq, w), lambda b, h, i: (b, 0, kb + h)),
                  pl.BlockSpec((None, seq, w), lambda b, h, i: (b, 0, vb + h)),
                  pl.BlockSpec((None, tq, w), lambda b, h, i: (b, i, gb + h)),
                  pl.BlockSpec((1, w), lambda b, h, i: (0, h))],
        out_specs=pl.BlockSpec((None, tq, w), lambda b, h, i: (b, i, h)),
        out_shape=jax.ShapeDtypeStruct((batch, seq, SB_HEADS * HEAD_DIM), BF16),
        compiler_params=pltpu.CompilerParams(
            dimension_semantics=("arbitrary", "arbitrary", "arbitrary"),
            vmem_limit_bytes=VMEM_LIMIT_BYTES),
        name="sb_attention",
    )(proj, proj, proj, proj, gain)


def _dsa_kernel(iq_ref, ik_ref, iw_ref, q_ref, k_ref, vt_ref, gate_ref, gain_ref, o_ref,
                sc_scr, qs_scr, s_scr, p_scr, *, tq, ck, topk, seq):
    i = pl.program_id(1)
    n_chunks = (i * tq) // ck + 1
    rep = DSA_HEADS // DSA_KV_HEADS
    kf = float(topk)
    groups = ck // SUBLANES

    iw_t = iw_ref[...].T
    q_pos = i * tq + lax.broadcasted_iota(I32, (ck, tq), 1)
    key_off = lax.broadcasted_iota(I32, (ck, tq), 0)

    def score_chunk(c, carry):
        mn, mx = carry
        start = pl.multiple_of(c * ck, ck)
        k_even = ik_ref[pl.ds(start, ck), 0:LANES]
        k_odd = ik_ref[pl.ds(start, ck), LANES:2 * LANES]
        acc = jnp.zeros((ck, tq), F32)
        for p in range(IDX_HEADS // 2):
            qp = iq_ref[:, p * LANES:(p + 1) * LANES]
            for hh, kk in ((2 * p, k_even), (2 * p + 1, k_odd)):
                acc = acc + iw_t[hh:hh + 1, :] * jnp.maximum(_dot_nt(kk, qp), 0.0)
        causal = (c * ck + key_off) <= q_pos
        sc_scr[c] = jnp.where(causal, acc, -jnp.inf)
        mx = jnp.maximum(mx, jnp.max(jnp.where(causal, acc, -jnp.inf), axis=0, keepdims=True))
        mn = jnp.minimum(mn, jnp.min(jnp.where(causal, acc, jnp.inf), axis=0, keepdims=True))
        return mn, mx

    row_min, row_max = lax.fori_loop(0, n_chunks, score_chunk,
                                     (jnp.full((1, tq), jnp.inf, F32), jnp.full((1, tq), -jnp.inf, F32)))

    def rows8(v):
        return jnp.broadcast_to(v, (SUBLANES, tq))

    def count_where(pred):
        def body(c, cnt):
            hit = jnp.where(pred(sc_scr[c].reshape(groups, SUBLANES, tq), c), 1.0, 0.0)
            return cnt + jnp.sum(hit.reshape(groups // COUNT_LANES, COUNT_LANES, SUBLANES, tq), axis=0)
        cnt = lax.fori_loop(0, n_chunks, body, jnp.zeros((COUNT_LANES, SUBLANES, tq), F32))
        return jnp.sum(jnp.sum(cnt, axis=0), axis=0, keepdims=True)

    n_valid = (i * tq + lax.broadcasted_iota(I32, (1, tq), 1) + 1).astype(F32)
    few = n_valid <= kf

    def bisect_pass(state):
        lo, hi, cnt_lo, done = state
        mid = 0.5 * lo + 0.5 * hi
        no_gap = jnp.logical_or(mid <= lo, mid >= hi)
        mid8 = rows8(mid)
        cnt = count_where(lambda s, c: s >= mid8)
        ge = cnt >= kf
        live = done < 0.5
        up = jnp.logical_and(live, ge)
        down = jnp.logical_and(live, jnp.logical_not(ge))
        lo = jnp.where(up, mid, lo)
        cnt_lo = jnp.where(up, cnt, cnt_lo)
        hi = jnp.where(down, mid, hi)
        done = jnp.where(jnp.logical_or(cnt_lo == kf, no_gap), 1.0, done)
        return lo, hi, cnt_lo, done

    def search_cond(carry):
        it, all_done = carry[0], carry[1]
        return jnp.logical_and(it < SEARCH_PASS_CAP, all_done < 0.5)

    def search_body(carry):
        state = carry[2:]
        for _ in range(SEARCH_UNROLL):
            state = bisect_pass(state)
        return (carry[0] + SEARCH_UNROLL, jnp.min(state[3])) + tuple(state)

    hi0 = row_max + (row_max - row_min) * (2.0 ** -10) + 1e-30
    done0 = jnp.where(few, 1.0, 0.0)
    carry = lax.while_loop(search_cond, search_body,
                           (jnp.int32(0), jnp.min(done0), row_min, hi0, n_valid, done0))
    thr, cnt_thr = carry[2], carry[4]
    thr8 = rows8(thr)

    excess = jnp.logical_and(cnt_thr > kf, jnp.logical_not(few))

    @pl.when(jnp.max(jnp.where(excess, 1.0, 0.0)) > 0.5)
    def _():
        need = kf - count_where(lambda s, c: s > thr8)
        key_in_chunk = lax.broadcasted_iota(I32, (groups, SUBLANES, tq), 0) * SUBLANES + lax.broadcasted_iota(
            I32, (groups, SUBLANES, tq), 1)

        def key_f(c):
            return (c * ck + key_in_chunk).astype(F32)

        def pos_pass(b, last_short):
            cand = last_short + jnp.left_shift(jnp.int32(1), b).astype(F32)
            cand8 = rows8(cand)
            cnt = count_where(lambda s, c: jnp.logical_and(s == thr8, key_f(c) <= cand8))
            return jnp.where(cnt < need, cand, last_short)

        n_bits = (seq - 1).bit_length()
        last_short = lax.fori_loop(0, n_bits, lambda t, x: pos_pass(n_bits - 1 - t, x),
                                   jnp.full((1, tq), -1.0, F32))
        keep_upto = rows8(jnp.where(excess, last_short + 1.0, float(seq)))

        def drop(c, carry):
            s = sc_scr[c].reshape(groups, SUBLANES, tq)
            cut = jnp.logical_and(s == thr8, key_f(c) > keep_upto)
            sc_scr[c] = jnp.where(cut, -jnp.inf, s).reshape(ck, tq)
            return carry

        lax.fori_loop(0, n_chunks, drop, 0)

    for g in range(DSA_KV_HEADS):
        for r in range(rep):
            h = g * rep + r
            qs_scr[g, r * tq:(r + 1) * tq, :] = q_ref[:, h * HEAD_DIM:(h + 1) * HEAD_DIM]
    def attend_chunk(c, stats):
        start = pl.multiple_of(c * ck, ck)
        bias = jnp.where(sc_scr[c] >= thr, 0.0, MASK_BIAS)
        bias = jnp.concatenate([bias] * rep, axis=1)
        out = []
        for g in range(DSA_KV_HEADS):
            m_old, l_old, acc_old = stats[g]
            kg = k_ref[pl.ds(start, ck), g * HEAD_DIM:(g + 1) * HEAD_DIM]
            vtg = vt_ref[c, g * VT_ROWS:(g + 1) * VT_ROWS, :]
            s = _dot_nt(kg, qs_scr[g]) + bias
            s_scr[g] = s
            m_new = jnp.maximum(m_old, jnp.max(s, axis=0, keepdims=True))
            alpha = jnp.exp2(m_old - m_new)
            p_scr[g] = jnp.exp2(s_scr[g] - m_new).astype(BF16)
            pv = _dot(vtg, p_scr[g])
            out.append((m_new, alpha * l_old + pv[HEAD_DIM:HEAD_DIM + 1],
                        alpha * acc_old + pv[:HEAD_DIM]))
        return tuple(out)

    init = tuple((jnp.full((1, rep * tq), MASK_BIAS, F32), jnp.zeros((1, rep * tq), F32),
                  jnp.zeros((HEAD_DIM, rep * tq), F32)) for _ in range(DSA_KV_HEADS))
    fin = lax.fori_loop(0, n_chunks, attend_chunk, init)

    for g in range(DSA_KV_HEADS):
        o_t = fin[g][2] * (1.0 / fin[g][1])
        for r in range(rep):
            sl = slice((g * rep + r) * HEAD_DIM, (g * rep + r + 1) * HEAD_DIM)
            o = o_t[:, r * tq:(r + 1) * tq].T
            o_ref[:, sl] = _norm_gate(o, gain_ref[:, sl], gate_ref[:, sl]).astype(o_ref.dtype)


def _dsa_attention(iq, ikp, iw, rproj, vt, pproj, gain, *, batch, seq, q_col, k_col, gate_col):
    tq, ck = DSA_QUERIES, DSA_KEYS
    assert seq % ck == 0 and seq % tq == 0
    topk = min(TOPK_MAX, seq // 4)
    dq = DSA_HEADS * HEAD_DIM
    dkv = DSA_KV_HEADS * HEAD_DIM
    rep = DSA_HEADS // DSA_KV_HEADS
    return pl.pallas_call(
        functools.partial(_dsa_kernel, tq=tq, ck=ck, topk=topk, seq=seq),
        grid=(batch, seq // tq),
        in_specs=[pl.BlockSpec((None, tq, IDX_HEADS * IDX_DIM), lambda b, i: (b, i, 0)),
                  pl.BlockSpec((None, seq, 2 * LANES), lambda b, i: (b, 0, 0)),
                  pl.BlockSpec((None, tq, LANES), lambda b, i: (b, i, 0)),
                  pl.BlockSpec((None, tq, dq), lambda b, i: (b, i, q_col // dq)),
                  pl.BlockSpec((None, seq, dkv), lambda b, i: (b, 0, k_col // dkv)),
                  pl.BlockSpec((None, seq // ck, DSA_KV_HEADS * VT_ROWS, ck), lambda b, i: (b, 0, 0, 0)),
                  pl.BlockSpec((None, tq, dq), lambda b, i: (b, i, gate_col // dq)),
                  pl.BlockSpec((1, dq), lambda b, i: (0, 0))],
        out_specs=pl.BlockSpec((None, tq, dq), lambda b, i: (b, i, 0)),
        out_shape=jax.ShapeDtypeStruct((batch, seq, dq), BF16),
        scratch_shapes=[pltpu.VMEM((seq // ck, ck, tq), F32),
                        pltpu.VMEM((DSA_KV_HEADS, rep * tq, HEAD_DIM), BF16),
                        pltpu.VMEM((DSA_KV_HEADS, ck, rep * tq), F32),
                        pltpu.VMEM((DSA_KV_HEADS, ck, rep * tq), BF16)],
        compiler_params=pltpu.CompilerParams(
            dimension_semantics=("arbitrary", "arbitrary"), vmem_limit_bytes=VMEM_LIMIT_BYTES),
        name="dsa_attention",
    )(iq, ikp, iw, rproj, rproj, vt, pproj, gain)


def _out_kernel(a_ref, b_ref, wa_ref, wb_ref, x_ref, g_ref, o_ref):
    y = x_ref[...] + _dot(a_ref[...], wa_ref[...]) + _dot(b_ref[...], wb_ref[...])
    ms = jnp.mean(y * y, axis=-1, keepdims=True)
    o_ref[...] = y * lax.rsqrt(ms + EPS) * g_ref[...]


def _out_proj(mix_a, mix_b, w_a, w_b, x2, final_g):
    m, d = x2.shape
    tm = OUT_ROWS
    ka, kb = mix_a.shape[1], mix_b.shape[1]
    return pl.pallas_call(
        _out_kernel,
        grid=(m // tm,),
        in_specs=[pl.BlockSpec((tm, ka), lambda i: (i, 0)),
                  pl.BlockSpec((tm, kb), lambda i: (i, 0)),
                  pl.BlockSpec((ka, d), lambda i: (0, 0)),
                  pl.BlockSpec((kb, d), lambda i: (0, 0)),
                  pl.BlockSpec((tm, d), lambda i: (i, 0)),
                  pl.BlockSpec((1, d), lambda i: (0, 0))],
        out_specs=pl.BlockSpec((tm, d), lambda i: (i, 0)),
        out_shape=jax.ShapeDtypeStruct((m, d), F32),
        compiler_params=pltpu.CompilerParams(
            dimension_semantics=("arbitrary",), vmem_limit_bytes=VMEM_LIMIT_BYTES),
        name="out_proj",
    )(mix_a, mix_b, w_a, w_b, x2, final_g)


def kernel(x, positions, norm_g, w_in, sb_out_g, dsa_out_g, w_out, final_g):
    batch, seq, d_model = x.shape
    d_sb = SB_HEADS * HEAD_DIM
    d_dsa = DSA_HEADS * HEAD_DIM
    d_kv = DSA_KV_HEADS * HEAD_DIM
    d_iq = IDX_HEADS * IDX_DIM
    depth = norm_g.shape[0]
    assert depth == 1 and d_sb + d_dsa == w_out.shape[1]
    m = batch * seq
    x2 = x.reshape(m, d_model)
    layer = 0

    sizes = (d_sb, d_sb, d_sb, d_sb, d_dsa, d_kv, d_kv, d_dsa, d_iq, IDX_DIM, IDX_HEADS)
    offs = [0]
    for s in sizes:
        offs.append(offs[-1] + s)
    wl = w_in[layer]
    (w_sbq, w_sbk, w_sbv, w_sbg, w_dq, w_dk, w_dv, w_dg, w_iq, w_ik, w_iw) = [
        wl[:, offs[t]:offs[t + 1]] for t in range(len(sizes))]

    tables = _rope_tables(positions)
    g_in = norm_g[layer].reshape(1, d_model)
    att_scale = HEAD_DIM ** -0.5

    w_r = jnp.concatenate([w_dq, w_dk], axis=1).astype(BF16)
    cs_r = jnp.concatenate([jnp.full((d_dsa,), att_scale * LOG2_E, F32), jnp.ones((d_kv,), F32)]).reshape(1, -1)
    n_r = w_r.shape[1] // LANES
    rproj, vt = _proj(x2, g_in, w_r, cs_r, tables, tn=w_r.shape[1] // 2,
                      kinds=("rope128",) * (n_r // 2), outs=((0, n_r // 2),), out_dtypes=(BF16,),
                      name="proj_rope", wt=w_dv.T.astype(BF16), t_chunk=DSA_KEYS)

    w_p = jnp.concatenate([w_dg, w_sbq, w_sbk, w_sbv, w_sbg], axis=1).astype(BF16)
    cs_p = jnp.concatenate([jnp.ones((d_dsa,), F32), jnp.full((d_sb,), att_scale, F32),
                            jnp.ones((3 * d_sb,), F32)]).reshape(1, -1)
    tn_p = 1024
    (pproj,) = _proj(x2, g_in, w_p, cs_p, None, tn=tn_p, kinds=("plain",) * (tn_p // LANES),
                     outs=((0, tn_p // LANES),), out_dtypes=(BF16,), name="proj_plain")

    zeros64 = jnp.zeros((d_model, IDX_DIM), w_in.dtype)
    w_i = jnp.concatenate([w_iq, w_ik, zeros64, zeros64, w_ik, w_iw,
                           jnp.zeros((d_model, LANES - IDX_HEADS), w_in.dtype)], axis=1).astype(BF16)
    n_i = w_i.shape[1] // LANES
    n_iq = d_iq // LANES
    cs_i = jnp.concatenate([jnp.ones((d_iq + 2 * LANES,), F32),
                            jnp.full((LANES,), (IDX_HEADS ** -0.5) * (IDX_DIM ** -0.5), F32)]).reshape(1, -1)
    kinds_i = ("rope64",) * (n_iq + 2) + ("plain",)
    iq, ikp, iw = _proj(x2, g_in, w_i, cs_i, tables, tn=w_i.shape[1], kinds=kinds_i,
                        outs=((0, n_iq), (n_iq, n_iq + 2), (n_iq + 2, n_i)),
                        out_dtypes=(BF16, BF16, F32), name="proj_index")

    rproj3 = rproj.reshape(batch, seq, -1)
    pproj3 = pproj.reshape(batch, seq, -1)
    c_sbq = d_dsa
    mix_sb = _sb_attention(pproj3, sb_out_g[layer].reshape(1, d_sb), batch=batch, seq=seq,
                           q_col=c_sbq, k_col=c_sbq + d_sb, v_col=c_sbq + 2 * d_sb, gate_col=c_sbq + 3 * d_sb)
    vt4 = vt.reshape(batch, seq // DSA_KEYS, DSA_KV_HEADS * VT_ROWS, DSA_KEYS)
    mix_dsa = _dsa_attention(iq.reshape(batch, seq, -1), ikp.reshape(batch, seq, -1), iw.reshape(batch, seq, -1),
                             rproj3, vt4, pproj3, dsa_out_g[layer].reshape(1, d_dsa), batch=batch, seq=seq,
                             q_col=0, k_col=d_dsa, gate_col=0)

    w_o = w_out[layer].astype(BF16)
    out = _out_proj(mix_sb.reshape(m, d_sb), mix_dsa.reshape(m, d_dsa), w_o[:d_sb], w_o[d_sb:], x2,
                    final_g.reshape(1, d_model))
    return out.reshape(batch, seq, d_model)
```

```python
import functools

import jax
import jax.numpy as jnp
from jax import lax
from jax.experimental import pallas as pl
from jax.experimental.pallas import tpu as pltpu

F32 = jnp.float32
BF16 = jnp.bfloat16
I32 = jnp.int32

HEAD_DIM = 128
SB_HEADS = 8
DSA_HEADS = 8
DSA_KV_HEADS = 2
IDX_HEADS = 16
IDX_DIM = 64
TOPK_MAX = 256
ROPE_THETA = 500000.0
ROPE_FRACTION_DIV = 4
EPS = 1e-6

LANES = 128
SUBLANES = 8
VMEM_LIMIT_BYTES = 56 * 1024 * 1024

PROJ_ROWS = 1024
PROJ_COLS = 768
OUT_ROWS = 512
SB_BLOCK = 256
SB_HEADS_PER_STEP = 4
DSA_QUERIES = 256
DSA_KEYS = 512
VT_ROWS = HEAD_DIM + 16

MASK_BIAS = -1e30
LOG2_E = 1.4426950408889634
SB_DEAD_LOG = -110.0
SEARCH_PASS_CAP = 320
SEARCH_UNROLL = 4
COUNT_LANES = 8


def _dot_nt(a, b):
    return lax.dot_general(a, b, (((1,), (1,)), ((), ())), preferred_element_type=F32)


def _dot(a, b):
    return jnp.dot(a, b, preferred_element_type=F32)


def _rope_table_kernel(pos_ref, invf_ref, sign_ref, c128_ref, s128_ref, c64_ref, s64_ref):
    pos = pos_ref[...].astype(F32)
    ang128 = pos * invf_ref[0:1, :]
    ang64 = pos * invf_ref[1:2, :]
    c128_ref[...] = jnp.cos(ang128)
    s128_ref[...] = jnp.sin(ang128) * sign_ref[0:1, :]
    c64_ref[...] = jnp.cos(ang64)
    s64_ref[...] = jnp.sin(ang64) * sign_ref[1:2, :]


def _rope_lane_patterns():
    lane = jnp.arange(LANES)
    half128 = HEAD_DIM // ROPE_FRACTION_DIV // 2
    half64 = IDX_DIM // ROPE_FRACTION_DIV // 2
    f128 = jnp.power(ROPE_THETA, -jnp.arange(half128, dtype=F32) / half128)
    f64 = jnp.power(ROPE_THETA, -jnp.arange(half64, dtype=F32) / half64)
    in128 = lane < 2 * half128
    invf128 = jnp.where(in128, f128[lane % half128], 0.0)
    sign128 = jnp.where(in128, jnp.where(lane < half128, -1.0, 1.0), 0.0)
    l64 = lane % IDX_DIM
    in64 = l64 < 2 * half64
    invf64 = jnp.where(in64, f64[l64 % half64], 0.0)
    sign64 = jnp.where(in64, jnp.where(l64 < half64, -1.0, 1.0), 0.0)
    return (jnp.stack([invf128, invf64]).astype(F32), jnp.stack([sign128, sign64]).astype(F32))


def _rope_tables(positions):
    rows = positions.size
    pos = positions.reshape(rows, 1).astype(I32)
    invf, sign = _rope_lane_patterns()
    tr = 1024
    tab = jax.ShapeDtypeStruct((rows, LANES), F32)
    return pl.pallas_call(
        _rope_table_kernel,
        grid=(rows // tr,),
        in_specs=[pl.BlockSpec((tr, 1), lambda r: (r, 0)),
                  pl.BlockSpec((2, LANES), lambda r: (0, 0)),
                  pl.BlockSpec((2, LANES), lambda r: (0, 0))],
        out_specs=[pl.BlockSpec((tr, LANES), lambda r: (r, 0))] * 4,
        out_shape=[tab] * 4,
        name="rope_tables",
    )(pos, invf, sign)


def _apply_rope(x, cos, sin, kind):
    lane = lax.broadcasted_iota(I32, x.shape, 1)
    if kind == "rope128":
        half = HEAD_DIM // ROPE_FRACTION_DIV // 2
        first = lane < half
    else:
        half = IDX_DIM // ROPE_FRACTION_DIV // 2
        first = (lane % IDX_DIM) < half
    partner = jnp.where(first, pltpu.roll(x, LANES - half, 1), pltpu.roll(x, half, 1))
    return x * cos + partner * sin


def _proj_kernel(x_ref, g_ref, w_ref, cs_ref, c128_ref, s128_ref, c64_ref, s64_ref, wtail_ref, wt_ref,
                 o_ref, ik_ref, iw_ref, vt_ref, h_scr, *, tile_kinds, iw_scale):
    n = pl.program_id(1)

    def finish(xj, kind):
        if kind == "rope128":
            return _apply_rope(xj, c128_ref[...], s128_ref[...], "rope128")
        if kind == "rope64":
            return _apply_rope(xj, c64_ref[...], s64_ref[...], "rope64")
        return xj

    @pl.when(n == 0)
    def _():
        xf = x_ref[...]
        ms = jnp.mean(xf * xf, axis=-1, keepdims=True)
        h_scr[...] = (xf * lax.rsqrt(ms + EPS) * g_ref[...]).astype(BF16)
        tail = _dot(h_scr[...], wtail_ref[...])
        for j in range(2):
            ik_ref[:, j * LANES:(j + 1) * LANES] = finish(
                tail[:, j * LANES:(j + 1) * LANES], "rope64").astype(ik_ref.dtype)
        iw_ref[...] = tail[:, 2 * LANES:3 * LANES] * iw_scale
        t = _dot_nt(wt_ref[...], h_scr[...])
        cw = vt_ref.shape[2]
        for c in range(vt_ref.shape[0]):
            for g in range(vt_ref.shape[1] // VT_ROWS):
                vt_ref[c, g * VT_ROWS:g * VT_ROWS + HEAD_DIM, :] = (
                    t[g * HEAD_DIM:(g + 1) * HEAD_DIM, c * cw:(c + 1) * cw].astype(vt_ref.dtype))
                vt_ref[c, g * VT_ROWS + HEAD_DIM:(g + 1) * VT_ROWS, :] = jnp.ones(
                    (VT_ROWS - HEAD_DIM, cw), vt_ref.dtype)

    for pattern, tiles in tile_kinds:
        @pl.when(functools.reduce(jnp.logical_or, [n == t for t in tiles]))
        def _(pattern=pattern):
            acc = _dot(h_scr[...], w_ref[...]) * cs_ref[...]
            for j, kind in enumerate(pattern):
                o_ref[:, j * LANES:(j + 1) * LANES] = finish(
                    acc[:, j * LANES:(j + 1) * LANES], kind).astype(o_ref.dtype)


def _proj(x2, g, w, colscale, tables, w_tail, wt, *, n_cols, tn, kinds, iw_scale, t_chunk):
    m, d = x2.shape
    tm = PROJ_ROWS
    per_tile = tn // LANES
    n_tiles = n_cols // tn
    patterns = {}
    for t in range(n_tiles):
        patterns.setdefault(tuple(kinds[t * per_tile:(t + 1) * per_tile]), []).append(t)
    t_rows = wt.shape[0] // HEAD_DIM * VT_ROWS
    row_tile = lambda i, j: (i, 0)
    whole = lambda i, j: (0, 0)
    return pl.pallas_call(
        functools.partial(_proj_kernel, tile_kinds=tuple(patterns.items()), iw_scale=iw_scale),
        grid=(m // tm, n_tiles),
        in_specs=[pl.BlockSpec((tm, d), row_tile),
                  pl.BlockSpec((1, d), whole),
                  pl.BlockSpec((d, tn), lambda i, j: (0, j)),
                  pl.BlockSpec((1, tn), lambda i, j: (0, j))]
        + [pl.BlockSpec((tm, LANES), row_tile)] * 4
        + [pl.BlockSpec(w_tail.shape, whole), pl.BlockSpec(wt.shape, whole)],
        out_specs=[pl.BlockSpec((tm, tn), lambda i, j: (i, j)),
                   pl.BlockSpec((tm, 2 * LANES), row_tile),
                   pl.BlockSpec((tm, LANES), row_tile),
                   pl.BlockSpec((tm // t_chunk, t_rows, t_chunk), lambda i, j: (i, 0, 0))],
        out_shape=[jax.ShapeDtypeStruct((m, n_cols), BF16),
                   jax.ShapeDtypeStruct((m, 2 * LANES), BF16),
                   jax.ShapeDtypeStruct((m, LANES), F32),
                   jax.ShapeDtypeStruct((m // t_chunk, t_rows, t_chunk), BF16)],
        scratch_shapes=[pltpu.VMEM((tm, d), BF16)],
        compiler_params=pltpu.CompilerParams(
            dimension_semantics=("arbitrary", "arbitrary"), vmem_limit_bytes=VMEM_LIMIT_BYTES),
        name="in_proj",
    )(x2, g, w, colscale, *tables, w_tail, wt)


def _norm_gate(o, gain, gate):
    ms = jnp.mean(o * o, axis=-1, keepdims=True)
    y = o * lax.rsqrt(ms + EPS) * gain
    gf = gate.astype(F32)
    return y * (gf * (1.0 / (1.0 + jnp.exp(-gf))))


def _sb_kernel(q_ref, k_ref, v_ref, gate_ref, gain_ref, o_ref, *, tq, hps):
    i = pl.program_id(2)
    row = lax.broadcasted_iota(I32, (tq, tq), 0)
    col = lax.broadcasted_iota(I32, (tq, tq), 1)
    later = jnp.where(row > col, 1.0, 0.0).astype(BF16)
    strict = col < row
    qs = [q_ref[:, h * HEAD_DIM:(h + 1) * HEAD_DIM] for h in range(hps)]

    def block(kb, cs, accs, diag):
        start = pl.multiple_of(kb * tq, tq)
        heads = range(hps)
        zs = [_dot_nt(qs[h], k_ref[pl.ds(start, tq), h * HEAD_DIM:(h + 1) * HEAD_DIM]) for h in heads]
        log_betas, log_1ms, his, los = [], [], [], []
        for h in heads:
            z = zs[h]
            sp = jnp.log(1.0 + jnp.exp(-jnp.abs(z)))
            log_beta = jnp.minimum(z, 0.0) - sp
            log_1m = log_beta - z
            if diag:
                log_beta = jnp.where(strict, log_beta, -jnp.inf)
                log_1m = jnp.where(strict, log_1m, 0.0)
            hi = log_1m.astype(BF16)
            log_betas.append(log_beta)
            log_1ms.append(log_1m)
            his.append(hi)
            los.append((log_1m - hi.astype(F32)).astype(BF16))
        tails = [_dot(his[h], later) + _dot(los[h], later) for h in heads]
        ws = [jnp.exp(log_betas[h] + tails[h] + cs[h]).astype(BF16) for h in heads]
        new_accs = [accs[h] + _dot(ws[h], v_ref[pl.ds(start, tq), h * HEAD_DIM:(h + 1) * HEAD_DIM])
                    for h in heads]
        new_cs = [cs[h] + jnp.sum(log_1ms[h], axis=1, keepdims=True) for h in heads]
        return new_cs, new_accs

    def dead_bound(cs):
        return functools.reduce(jnp.maximum, [jnp.max(c) for c in cs])

    cs0 = [jnp.zeros((tq, 1), F32) for _ in range(hps)]
    accs0 = [jnp.zeros((tq, HEAD_DIM), F32) for _ in range(hps)]
    cs1, accs1 = block(i, cs0, accs0, True)

    def cond(carry):
        kb, bound, _, _ = carry
        return jnp.logical_and(kb >= 0, bound > SB_DEAD_LOG)

    def body(carry):
        kb, _, cs, accs = carry
        cs, accs = block(kb, list(cs), list(accs), False)
        return kb - 1, dead_bound(cs), tuple(cs), tuple(accs)

    _, _, _, accs = lax.while_loop(cond, body, (i - 1, dead_bound(cs1), tuple(cs1), tuple(accs1)))

    for h in range(hps):
        sl = slice(h * HEAD_DIM, (h + 1) * HEAD_DIM)
        o_ref[:, sl] = _norm_gate(accs[h], gain_ref[:, sl], gate_ref[:, sl]).astype(o_ref.dtype)


def _sb_attention(proj, gain, *, batch, seq, q_col, k_col, v_col, gate_col):
    tq, hps = SB_BLOCK, SB_HEADS_PER_STEP
    w = hps * HEAD_DIM
    qb, kb, vb, gb = (c // w for c in (q_col, k_col, v_col, gate_col))
    return pl.pallas_call(
        functools.partial(_sb_kernel, tq=tq, hps=hps),
        grid=(batch, SB_HEADS // hps, seq // tq),
        in_specs=[pl.BlockSpec((None, tq, w), lambda b, h, i: (b, i, qb + h)),
                  pl.BlockSpec((None, seq, w), lambda b, h, i: (b, 0, kb + h)),
                  pl.BlockSpec((None, seq, w), lambda b, h, i: (b, 0, vb + h)),
                  pl.BlockSpec((None, tq, w), lambda b, h, i: (b, i, gb + h)),
                  pl.BlockSpec((1, w), lambda b, h, i: (0, h))],
        out_specs=pl.BlockSpec((None, tq, w), lambda b, h, i: (b, i, h)),
        out_shape=jax.ShapeDtypeStruct((batch, seq, SB_HEADS * HEAD_DIM), BF16),
        compiler_params=pltpu.CompilerParams(
            dimension_semantics=("arbitrary", "arbitrary", "arbitrary"),
            vmem_limit_bytes=VMEM_LIMIT_BYTES),
        name="sb_attention",
    )(proj, proj, proj, proj, gain)


def _dsa_kernel(iq_lo_ref, iq_hi_ref, ik_ref, iw_ref, q_ref, k_ref, vt_ref, gate_lo_ref, gate_hi_ref, gain_ref,
                o_ref, sc_scr, qs_scr, s_scr, p_scr, *, tq, ck, topk, seq):
    i = pl.program_id(1)
    n_chunks = (i * tq) // ck + 1
    rep = DSA_HEADS // DSA_KV_HEADS
    kf = float(topk)
    groups = ck // SUBLANES

    iw_t = iw_ref[...].T
    q_pos = i * tq + lax.broadcasted_iota(I32, (ck, tq), 1)
    key_off = lax.broadcasted_iota(I32, (ck, tq), 0)

    def score_chunk(c, carry):
        mn, mx = carry
        start = pl.multiple_of(c * ck, ck)
        k_even = ik_ref[pl.ds(start, ck), 0:LANES]
        k_odd = ik_ref[pl.ds(start, ck), LANES:2 * LANES]
        acc = jnp.zeros((ck, tq), F32)
        pairs = IDX_HEADS // 2
        for p in range(pairs):
            half_ref, pp = (iq_lo_ref, p) if p < pairs // 2 else (iq_hi_ref, p - pairs // 2)
            qp = half_ref[:, pp * LANES:(pp + 1) * LANES]
            for hh, kk in ((2 * p, k_even), (2 * p + 1, k_odd)):
                acc = acc + iw_t[hh:hh + 1, :] * jnp.maximum(_dot_nt(kk, qp), 0.0)
        causal = (c * ck + key_off) <= q_pos
        sc_scr[c] = jnp.where(causal, acc, -jnp.inf)
        mx = jnp.maximum(mx, jnp.max(jnp.where(causal, acc, -jnp.inf), axis=0, keepdims=True))
        mn = jnp.minimum(mn, jnp.min(jnp.where(causal, acc, jnp.inf), axis=0, keepdims=True))
        return mn, mx

    row_min, row_max = lax.fori_loop(0, n_chunks, score_chunk,
                                     (jnp.full((1, tq), jnp.inf, F32), jnp.full((1, tq), -jnp.inf, F32)))

    def rows8(v):
        return jnp.broadcast_to(v, (SUBLANES, tq))

    def count_where(pred):
        def body(c, cnt):
            hit = jnp.where(pred(sc_scr[c].reshape(groups, SUBLANES, tq), c), 1.0, 0.0)
            return cnt + jnp.sum(hit.reshape(groups // COUNT_LANES, COUNT_LANES, SUBLANES, tq), axis=0)
        cnt = lax.fori_loop(0, n_chunks, body, jnp.zeros((COUNT_LANES, SUBLANES, tq), F32))
        return jnp.sum(jnp.sum(cnt, axis=0), axis=0, keepdims=True)

    def count_ge(mid):
        mid8 = rows8(mid)

        def body(c, cnt):
            cnt = list(cnt)
            for j in range(groups):
                blk = sc_scr[c, j * SUBLANES:(j + 1) * SUBLANES, :]
                cnt[j % COUNT_LANES] = cnt[j % COUNT_LANES] + jnp.where(blk >= mid8, 1.0, 0.0)
            return tuple(cnt)

        cnt = lax.fori_loop(0, n_chunks, body, tuple(jnp.zeros((SUBLANES, tq), F32) for _ in range(COUNT_LANES)))
        return jnp.sum(functools.reduce(lambda a, b: a + b, cnt), axis=0, keepdims=True)

    n_valid = (i * tq + lax.broadcasted_iota(I32, (1, tq), 1) + 1).astype(F32)
    few = n_valid <= kf

    def bisect_pass(state):
        lo, hi, cnt_lo, done = state
        mid = 0.5 * lo + 0.5 * hi
        no_gap = jnp.logical_or(mid <= lo, mid >= hi)
        cnt = count_ge(mid)
        ge = cnt >= kf
        live = done < 0.5
        up = jnp.logical_and(live, ge)
        down = jnp.logical_and(live, jnp.logical_not(ge))
        lo = jnp.where(up, mid, lo)
        cnt_lo = jnp.where(up, cnt, cnt_lo)
        hi = jnp.where(down, mid, hi)
        done = jnp.where(jnp.logical_or(cnt_lo == kf, no_gap), 1.0, done)
        return lo, hi, cnt_lo, done

    def search_cond(carry):
        it, all_done = carry[0], carry[1]
        return jnp.logical_and(it < SEARCH_PASS_CAP, all_done < 0.5)

    def search_body(carry):
        state = carry[2:]
        for _ in range(SEARCH_UNROLL):
            state = bisect_pass(state)
        return (carry[0] + SEARCH_UNROLL, jnp.min(state[3])) + tuple(state)

    hi0 = row_max + (row_max - row_min) * (2.0 ** -10) + 1e-30
    done0 = jnp.where(few, 1.0, 0.0)
    carry = lax.while_loop(search_cond, search_body,
                           (jnp.int32(0), jnp.min(done0), row_min, hi0, n_valid, done0))
    thr, cnt_thr = carry[2], carry[4]
    thr8 = rows8(thr)

    excess = jnp.logical_and(cnt_thr > kf, jnp.logical_not(few))

    @pl.when(jnp.max(jnp.where(excess, 1.0, 0.0)) > 0.5)
    def _():
        need = kf - count_where(lambda s, c: s > thr8)
        key_in_chunk = lax.broadcasted_iota(I32, (groups, SUBLANES, tq), 0) * SUBLANES + lax.broadcasted_iota(
            I32, (groups, SUBLANES, tq), 1)

        def key_f(c):
            return (c * ck + key_in_chunk).astype(F32)

        def pos_pass(b, last_short):
            cand = last_short + jnp.left_shift(jnp.int32(1), b).astype(F32)
            cand8 = rows8(cand)
            cnt = count_where(lambda s, c: jnp.logical_and(s == thr8, key_f(c) <= cand8))
            return jnp.where(cnt < need, cand, last_short)

        n_bits = (seq - 1).bit_length()
        last_short = lax.fori_loop(0, n_bits, lambda t, x: pos_pass(n_bits - 1 - t, x),
                                   jnp.full((1, tq), -1.0, F32))
        keep_upto = rows8(jnp.where(excess, last_short + 1.0, float(seq)))

        def drop(c, carry):
            s = sc_scr[c].reshape(groups, SUBLANES, tq)
            cut = jnp.logical_and(s == thr8, key_f(c) > keep_upto)
            sc_scr[c] = jnp.where(cut, -jnp.inf, s).reshape(ck, tq)
            return carry

        lax.fori_loop(0, n_chunks, drop, 0)

    for g in range(DSA_KV_HEADS):
        for r in range(rep):
            h = g * rep + r
            qs_scr[g, r * tq:(r + 1) * tq, :] = q_ref[:, h * HEAD_DIM:(h + 1) * HEAD_DIM]
    def attend_chunk(c, stats):
        start = pl.multiple_of(c * ck, ck)
        bias = jnp.where(sc_scr[c] >= thr, 0.0, MASK_BIAS)
        bias = jnp.concatenate([bias] * rep, axis=1)
        kv = range(DSA_KV_HEADS)
        m_news = []
        for g in kv:
            kg = k_ref[pl.ds(start, ck), g * HEAD_DIM:(g + 1) * HEAD_DIM]
            s = _dot_nt(kg, qs_scr[g]) + bias
            s_scr[g] = s
            m_news.append(jnp.maximum(stats[g][0], jnp.max(s, axis=0, keepdims=True)))
        for g in kv:
            p_scr[g] = jnp.exp2(s_scr[g] - m_news[g]).astype(BF16)
        out = []
        for g in kv:
            m_old, l_old, acc_old = stats[g]
            alpha = jnp.exp2(m_old - m_news[g])
            pv = _dot(vt_ref[c, g * VT_ROWS:(g + 1) * VT_ROWS, :], p_scr[g])
            out.append((m_news[g], alpha * l_old + pv[HEAD_DIM:HEAD_DIM + 1],
                        alpha * acc_old + pv[:HEAD_DIM]))
        return tuple(out)

    init = tuple((jnp.full((1, rep * tq), MASK_BIAS, F32), jnp.zeros((1, rep * tq), F32),
                  jnp.zeros((HEAD_DIM, rep * tq), F32)) for _ in range(DSA_KV_HEADS))
    fin = lax.fori_loop(0, n_chunks, attend_chunk, init)

    for g in range(DSA_KV_HEADS):
        o_t = fin[g][2] * (1.0 / fin[g][1])
        gate_ref = (gate_lo_ref, gate_hi_ref)[g]
        for r in range(rep):
            sl = slice((g * rep + r) * HEAD_DIM, (g * rep + r + 1) * HEAD_DIM)
            gsl = slice(r * HEAD_DIM, (r + 1) * HEAD_DIM)
            o = o_t[:, r * tq:(r + 1) * tq].T
            o_ref[:, sl] = _norm_gate(o, gain_ref[:, sl], gate_ref[:, gsl]).astype(o_ref.dtype)


def _dsa_attention(proj, ikp, iw, vt, gain, *, batch, seq, iq_col, q_col, k_col, gate_col):
    tq, ck = DSA_QUERIES, DSA_KEYS
    assert seq % ck == 0 and seq % tq == 0
    topk = min(TOPK_MAX, seq // 4)
    dq = DSA_HEADS * HEAD_DIM
    dkv = DSA_KV_HEADS * HEAD_DIM
    rep = DSA_HEADS // DSA_KV_HEADS
    half = dq // 2
    assert IDX_HEADS * IDX_DIM == dq and all(c % half == 0 for c in (iq_col, gate_col))
    assert q_col % dq == 0 and k_col % dkv == 0

    def half_block(col, which):
        return pl.BlockSpec((None, tq, half), lambda b, i: (b, i, col // half + which))

    return pl.pallas_call(
        functools.partial(_dsa_kernel, tq=tq, ck=ck, topk=topk, seq=seq),
        grid=(batch, seq // tq),
        in_specs=[half_block(iq_col, 0), half_block(iq_col, 1),
                  pl.BlockSpec((None, seq, 2 * LANES), lambda b, i: (b, 0, 0)),
                  pl.BlockSpec((None, tq, LANES), lambda b, i: (b, i, 0)),
                  pl.BlockSpec((None, tq, dq), lambda b, i: (b, i, q_col // dq)),
                  pl.BlockSpec((None, seq, dkv), lambda b, i: (b, 0, k_col // dkv)),
                  pl.BlockSpec((None, seq // ck, DSA_KV_HEADS * VT_ROWS, ck), lambda b, i: (b, 0, 0, 0)),
                  half_block(gate_col, 0), half_block(gate_col, 1),
                  pl.BlockSpec((1, dq), lambda b, i: (0, 0))],
        out_specs=pl.BlockSpec((None, tq, dq), lambda b, i: (b, i, 0)),
        out_shape=jax.ShapeDtypeStruct((batch, seq, dq), BF16),
        scratch_shapes=[pltpu.VMEM((seq // ck, ck, tq), F32),
                        pltpu.VMEM((DSA_KV_HEADS, rep * tq, HEAD_DIM), BF16),
                        pltpu.VMEM((DSA_KV_HEADS, ck, rep * tq), F32),
                        pltpu.VMEM((DSA_KV_HEADS, ck, rep * tq), BF16)],
        compiler_params=pltpu.CompilerParams(
            dimension_semantics=("arbitrary", "arbitrary"), vmem_limit_bytes=VMEM_LIMIT_BYTES),
        name="dsa_attention",
    )(proj, proj, ikp, iw, proj, proj, vt, proj, proj, gain)


def _out_kernel(a_ref, b_ref, wa_ref, wb_ref, x_ref, g_ref, o_ref):
    y = x_ref[...] + _dot(a_ref[...], wa_ref[...]) + _dot(b_ref[...], wb_ref[...])
    ms = jnp.mean(y * y, axis=-1, keepdims=True)
    o_ref[...] = y * lax.rsqrt(ms + EPS) * g_ref[...]


def _out_proj(mix_a, mix_b, w_a, w_b, x2, final_g):
    m, d = x2.shape
    tm = OUT_ROWS
    ka, kb = mix_a.shape[1], mix_b.shape[1]
    return pl.pallas_call(
        _out_kernel,
        grid=(m // tm,),
        in_specs=[pl.BlockSpec((tm, ka), lambda i: (i, 0)),
                  pl.BlockSpec((tm, kb), lambda i: (i, 0)),
                  pl.BlockSpec((ka, d), lambda i: (0, 0)),
                  pl.BlockSpec((kb, d), lambda i: (0, 0)),
                  pl.BlockSpec((tm, d), lambda i: (i, 0)),
                  pl.BlockSpec((1, d), lambda i: (0, 0))],
        out_specs=pl.BlockSpec((tm, d), lambda i: (i, 0)),
        out_shape=jax.ShapeDtypeStruct((m, d), F32),
        compiler_params=pltpu.CompilerParams(
            dimension_semantics=("arbitrary",), vmem_limit_bytes=VMEM_LIMIT_BYTES),
        name="out_proj",
    )(mix_a, mix_b, w_a, w_b, x2, final_g)


def kernel(x, positions, norm_g, w_in, sb_out_g, dsa_out_g, w_out, final_g):
    batch, seq, d_model = x.shape
    d_sb = SB_HEADS * HEAD_DIM
    d_dsa = DSA_HEADS * HEAD_DIM
    d_kv = DSA_KV_HEADS * HEAD_DIM
    d_iq = IDX_HEADS * IDX_DIM
    depth = norm_g.shape[0]
    assert depth == 1 and d_sb + d_dsa == w_out.shape[1]
    m = batch * seq
    x2 = x.reshape(m, d_model)
    layer = 0

    names = ("sb_q", "sb_k", "sb_v", "sb_gate", "dsa_q", "dsa_k", "dsa_v", "dsa_gate", "idx_q", "idx_k", "idx_w")
    sizes = (d_sb, d_sb, d_sb, d_sb, d_dsa, d_kv, d_kv, d_dsa, d_iq, IDX_DIM, IDX_HEADS)
    col, off = {}, 0
    for name, size in zip(names, sizes):
        col[name] = off
        off += size
    n_main = col["idx_k"]
    assert n_main % PROJ_COLS == 0

    w_bf = w_in[layer].astype(BF16)
    w_ik = w_bf[:, col["idx_k"]:col["idx_k"] + IDX_DIM]
    zeros64 = jnp.zeros((d_model, LANES - IDX_DIM), BF16)
    w_tail = jnp.concatenate([w_ik, zeros64, zeros64, w_ik, w_bf[:, col["idx_w"]:col["idx_w"] + IDX_HEADS],
                              jnp.zeros((d_model, LANES - IDX_HEADS), BF16)], axis=1)
    w_vt = w_bf[:, col["dsa_v"]:col["dsa_v"] + d_kv].T

    att_scale = HEAD_DIM ** -0.5
    chunk = jnp.arange(n_main) // LANES
    colscale = jnp.where(chunk < d_sb // LANES, att_scale, 1.0)
    colscale = jnp.where((chunk >= col["dsa_q"] // LANES) & (chunk < col["dsa_k"] // LANES),
                         att_scale * LOG2_E, colscale).astype(F32).reshape(1, n_main)
    kinds = ["plain"] * (n_main // LANES)
    for j in range(col["dsa_q"] // LANES, col["dsa_v"] // LANES):
        kinds[j] = "rope128"
    for j in range(col["idx_q"] // LANES, col["idx_k"] // LANES):
        kinds[j] = "rope64"

    tables = _rope_tables(positions)
    proj, ikp, iw, vt = _proj(x2, norm_g[layer].reshape(1, d_model), w_bf, colscale, tables, w_tail, w_vt,
                              n_cols=n_main, tn=PROJ_COLS, kinds=tuple(kinds),
                              iw_scale=(IDX_HEADS ** -0.5) * (IDX_DIM ** -0.5), t_chunk=DSA_KEYS)

    proj3 = proj.reshape(batch, seq, n_main)
    mix_sb = _sb_attention(proj3, sb_out_g[layer].reshape(1, d_sb), batch=batch, seq=seq,
                           q_col=col["sb_q"], k_col=col["sb_k"], v_col=col["sb_v"], gate_col=col["sb_gate"])
    vt4 = vt.reshape(batch, seq // DSA_KEYS, DSA_KV_HEADS * VT_ROWS, DSA_KEYS)
    mix_dsa = _dsa_attention(proj3, ikp.reshape(batch, seq, -1), iw.reshape(batch, seq, -1), vt4,
                             dsa_out_g[layer].reshape(1, d_dsa), batch=batch, seq=seq,
                             iq_col=col["idx_q"], q_col=col["dsa_q"], k_col=col["dsa_k"], gate_col=col["dsa_gate"])

    w_o = w_out[layer].astype(BF16)
    out = _out_proj(mix_sb.reshape(m, d_sb), mix_dsa.reshape(m, d_dsa), w_o[:d_sb], w_o[d_sb:], x2,
                    final_g.reshape(1, d_model))
    return out.reshape(batch, seq, d_model)
```

```python
import functools

import jax
import jax.numpy as jnp
from jax import lax
from jax.experimental import pallas as pl
from jax.experimental.pallas import tpu as pltpu

F32 = jnp.float32
BF16 = jnp.bfloat16
I32 = jnp.int32

HEAD_DIM = 128
SB_HEADS = 8
DSA_HEADS = 8
DSA_KV_HEADS = 2
IDX_HEADS = 16
IDX_DIM = 64
TOPK_MAX = 256
ROPE_THETA = 500000.0
ROPE_FRACTION_DIV = 4
EPS = 1e-6

LANES = 128
SUBLANES = 8
VMEM_LIMIT_BYTES = 56 * 1024 * 1024

PROJ_ROWS = 1024
PROJ_COLS = 1280
OUT_ROWS = 512
SB_BLOCK = 256
SB_HEADS_PER_STEP = 4
DSA_QUERIES = 512
DSA_KEYS = 512
VT_ROWS = HEAD_DIM + 16

MASK_BIAS = -1e30
LOG2_E = 1.4426950408889634
SB_DEAD_LOG = -110.0
SEARCH_PASS_CAP = 320
SEARCH_UNROLL = 4
COUNT_LANES = 8


def _dot_nt(a, b):
    return lax.dot_general(a, b, (((1,), (1,)), ((), ())), preferred_element_type=F32)


def _dot(a, b):
    return jnp.dot(a, b, preferred_element_type=F32)


ROT128 = HEAD_DIM // ROPE_FRACTION_DIV
ROT64 = IDX_DIM // ROPE_FRACTION_DIV


def _rope_table_kernel(pos_ref, pat_ref, c128_ref, s128_ref, c64_ref, s64_ref):
    pos = pos_ref[...].astype(F32)
    ang = pos * pat_ref[0:1, :]
    cb = jnp.cos(ang)
    sb = jnp.sin(ang) * pat_ref[1:2, :]
    lane = lax.broadcasted_iota(I32, ang.shape, 1)
    wide = lane < ROT128
    c128_ref[...] = jnp.where(wide, cb, 1.0)
    s128_ref[...] = jnp.where(wide, sb, 0.0)
    rotated = (lane % IDX_DIM) < ROT64
    low = lane < IDX_DIM
    c64_ref[...] = jnp.where(rotated, jnp.where(low, pltpu.roll(cb, LANES - ROT128, 1),
                                                pltpu.roll(cb, IDX_DIM - ROT128, 1)), 1.0)
    s64_ref[...] = jnp.where(rotated, jnp.where(low, pltpu.roll(sb, LANES - ROT128, 1),
                                                pltpu.roll(sb, IDX_DIM - ROT128, 1)), 0.0)


def _rope_lane_patterns():
    lane = jnp.arange(LANES)
    f128 = jnp.power(ROPE_THETA, -jnp.arange(ROT128 // 2, dtype=F32) / (ROT128 // 2))
    f64 = jnp.power(ROPE_THETA, -jnp.arange(ROT64 // 2, dtype=F32) / (ROT64 // 2))
    in128 = lane < ROT128
    l64 = lane - ROT128
    in64 = (l64 >= 0) & (l64 < ROT64)
    invf = jnp.where(in128, f128[lane % (ROT128 // 2)], jnp.where(in64, f64[l64 % (ROT64 // 2)], 0.0))
    sign = jnp.where(in128, jnp.where(lane < ROT128 // 2, -1.0, 1.0),
                     jnp.where(in64, jnp.where(l64 < ROT64 // 2, -1.0, 1.0), 0.0))
    return jnp.stack([invf, sign]).astype(F32)


def _rope_tables(positions):
    rows = positions.size
    pos = positions.reshape(rows, 1).astype(I32)
    tr = PROJ_ROWS
    tab = jax.ShapeDtypeStruct((rows, LANES), F32)
    return pl.pallas_call(
        _rope_table_kernel,
        grid=(rows // tr,),
        in_specs=[pl.BlockSpec((tr, 1), lambda r: (r, 0)),
                  pl.BlockSpec((2, LANES), lambda r: (0, 0))],
        out_specs=[pl.BlockSpec((tr, LANES), lambda r: (r, 0))] * 4,
        out_shape=[tab] * 4,
        name="rope_tables",
    )(pos, _rope_lane_patterns())


def _apply_rope(x, cos, sin, kind):
    lane = lax.broadcasted_iota(I32, x.shape, 1)
    if kind == "rope128":
        half = HEAD_DIM // ROPE_FRACTION_DIV // 2
        first = lane < half
    else:
        half = IDX_DIM // ROPE_FRACTION_DIV // 2
        first = (lane % IDX_DIM) < half
    partner = jnp.where(first, pltpu.roll(x, LANES - half, 1), pltpu.roll(x, half, 1))
    return x * cos + partner * sin


def _proj_kernel(x_ref, g_ref, w_ref, cs_ref, c128_ref, s128_ref, c64_ref, s64_ref, wtail_ref, wt_ref,
                 o_ref, ik_ref, iw_ref, vt_ref, h_scr, *, tile_kinds, iw_scale):
    n = pl.program_id(1)

    def finish(xj, kind):
        if kind == "rope128":
            return _apply_rope(xj, c128_ref[...], s128_ref[...], "rope128")
        if kind == "rope64":
            return _apply_rope(xj, c64_ref[...], s64_ref[...], "rope64")
        return xj

    @pl.when(n == 0)
    def _():
        xf = x_ref[...]
        ms = jnp.mean(xf * xf, axis=-1, keepdims=True)
        h_scr[...] = (xf * lax.rsqrt(ms + EPS) * g_ref[...]).astype(BF16)
        tail = _dot(h_scr[...], wtail_ref[...])
        for j in range(2):
            ik_ref[:, j * LANES:(j + 1) * LANES] = finish(
                tail[:, j * LANES:(j + 1) * LANES], "rope64").astype(ik_ref.dtype)
        iw_ref[...] = tail[:, 2 * LANES:3 * LANES] * iw_scale
        t = _dot_nt(wt_ref[...], h_scr[...])
        cw = vt_ref.shape[2]
        for c in range(vt_ref.shape[0]):
            for g in range(vt_ref.shape[1] // VT_ROWS):
                vt_ref[c, g * VT_ROWS:g * VT_ROWS + HEAD_DIM, :] = (
                    t[g * HEAD_DIM:(g + 1) * HEAD_DIM, c * cw:(c + 1) * cw].astype(vt_ref.dtype))
                vt_ref[c, g * VT_ROWS + HEAD_DIM:(g + 1) * VT_ROWS, :] = jnp.ones(
                    (VT_ROWS - HEAD_DIM, cw), vt_ref.dtype)

    for pattern, tiles in tile_kinds:
        @pl.when(functools.reduce(jnp.logical_or, [n == t for t in tiles]))
        def _(pattern=pattern):
            acc = _dot(h_scr[...], w_ref[...]) * cs_ref[...]
            for j, kind in enumerate(pattern):
                o_ref[:, j * LANES:(j + 1) * LANES] = finish(
                    acc[:, j * LANES:(j + 1) * LANES], kind).astype(o_ref.dtype)


def _proj(x2, g, w, colscale, tables, w_tail, wt, *, n_cols, tn, kinds, iw_scale, t_chunk):
    m, d = x2.shape
    tm = PROJ_ROWS
    per_tile = tn // LANES
    n_tiles = n_cols // tn
    patterns = {}
    for t in range(n_tiles):
        patterns.setdefault(tuple(kinds[t * per_tile:(t + 1) * per_tile]), []).append(t)
    t_rows = wt.shape[0] // HEAD_DIM * VT_ROWS
    row_tile = lambda i, j: (i, 0)
    whole = lambda i, j: (0, 0)
    return pl.pallas_call(
        functools.partial(_proj_kernel, tile_kinds=tuple(patterns.items()), iw_scale=iw_scale),
        grid=(m // tm, n_tiles),
        in_specs=[pl.BlockSpec((tm, d), row_tile),
                  pl.BlockSpec((1, d), whole),
                  pl.BlockSpec((d, tn), lambda i, j: (0, j)),
                  pl.BlockSpec((1, tn), lambda i, j: (0, j))]
        + [pl.BlockSpec((tm, LANES), row_tile)] * 4
        + [pl.BlockSpec(w_tail.shape, whole), pl.BlockSpec(wt.shape, whole)],
        out_specs=[pl.BlockSpec((tm, tn), lambda i, j: (i, j)),
                   pl.BlockSpec((tm, 2 * LANES), row_tile),
                   pl.BlockSpec((tm, LANES), row_tile),
                   pl.BlockSpec((tm // t_chunk, t_rows, t_chunk), lambda i, j: (i, 0, 0))],
        out_shape=[jax.ShapeDtypeStruct((m, n_cols), BF16),
                   jax.ShapeDtypeStruct((m, 2 * LANES), BF16),
                   jax.ShapeDtypeStruct((m, LANES), F32),
                   jax.ShapeDtypeStruct((m // t_chunk, t_rows, t_chunk), BF16)],
        scratch_shapes=[pltpu.VMEM((tm, d), BF16)],
        compiler_params=pltpu.CompilerParams(
            dimension_semantics=("arbitrary", "arbitrary"), vmem_limit_bytes=VMEM_LIMIT_BYTES),
        name="in_proj",
    )(x2, g, w, colscale, *tables, w_tail, wt)


def _unit_rms(o, axis):
    return o * lax.rsqrt(jnp.mean(o * o, axis=axis, keepdims=True) + EPS)


def _gain_gate(y, gain, gate):
    gf = gate.astype(F32)
    return y * gain * (gf * (1.0 / (1.0 + jnp.exp(-gf))))


def _norm_gate(o, gain, gate):
    return _gain_gate(_unit_rms(o, -1), gain, gate)


def _sb_kernel(q_ref, k_ref, v_ref, gate_ref, gain_ref, o_ref, *, tq, hps):
    i = pl.program_id(2)
    row = lax.broadcasted_iota(I32, (tq, tq), 0)
    col = lax.broadcasted_iota(I32, (tq, tq), 1)
    later = jnp.where(row > col, 1.0, 0.0).astype(BF16)
    strict = col < row
    qs = [q_ref[:, h * HEAD_DIM:(h + 1) * HEAD_DIM] for h in range(hps)]

    def block(kb, cs, accs, diag):
        start = pl.multiple_of(kb * tq, tq)
        heads = range(hps)
        zs = [_dot_nt(qs[h], k_ref[pl.ds(start, tq), h * HEAD_DIM:(h + 1) * HEAD_DIM]) for h in heads]
        log_betas, log_1ms, his, los = [], [], [], []
        for h in heads:
            z = zs[h]
            sp = jnp.log(1.0 + jnp.exp(-jnp.abs(z)))
            log_beta = jnp.minimum(z, 0.0) - sp
            log_1m = log_beta - z
            if diag:
                log_beta = jnp.where(strict, log_beta, -jnp.inf)
                log_1m = jnp.where(strict, log_1m, 0.0)
            hi = log_1m.astype(BF16)
            log_betas.append(log_beta)
            log_1ms.append(log_1m)
            his.append(hi)
            los.append((log_1m - hi.astype(F32)).astype(BF16))
        tails = [_dot(his[h], later) + _dot(los[h], later) for h in heads]
        ws = [jnp.exp(log_betas[h] + tails[h] + cs[h]).astype(BF16) for h in heads]
        new_accs = [accs[h] + _dot(ws[h], v_ref[pl.ds(start, tq), h * HEAD_DIM:(h + 1) * HEAD_DIM])
                    for h in heads]
        new_cs = [cs[h] + jnp.sum(log_1ms[h], axis=1, keepdims=True) for h in heads]
        return new_cs, new_accs

    def dead_bound(cs):
        return functools.reduce(jnp.maximum, [jnp.max(c) for c in cs])

    cs0 = [jnp.zeros((tq, 1), F32) for _ in range(hps)]
    accs0 = [jnp.zeros((tq, HEAD_DIM), F32) for _ in range(hps)]
    cs1, accs1 = block(i, cs0, accs0, True)

    def cond(carry):
        kb, bound, _, _ = carry
        return jnp.logical_and(kb >= 0, bound > SB_DEAD_LOG)

    def body(carry):
        kb, _, cs, accs = carry
        cs, accs = block(kb, list(cs), list(accs), False)
        return kb - 1, dead_bound(cs), tuple(cs), tuple(accs)

    _, _, _, accs = lax.while_loop(cond, body, (i - 1, dead_bound(cs1), tuple(cs1), tuple(accs1)))

    for h in range(hps):
        sl = slice(h * HEAD_DIM, (h + 1) * HEAD_DIM)
        o_ref[:, sl] = _norm_gate(accs[h], gain_ref[:, sl], gate_ref[:, sl]).astype(o_ref.dtype)


def _sb_attention(proj, gain, *, batch, seq, q_col, k_col, v_col, gate_col):
    tq, hps = SB_BLOCK, SB_HEADS_PER_STEP
    w = hps * HEAD_DIM
    qb, kb, vb, gb = (c // w for c in (q_col, k_col, v_col, gate_col))
    return pl.pallas_call(
        functools.partial(_sb_kernel, tq=tq, hps=hps),
        grid=(batch, SB_HEADS // hps, seq // tq),
        in_specs=[pl.BlockSpec((None, tq, w), lambda b, h, i: (b, i, qb + h)),
                  pl.BlockSpec((None, seq, w), lambda b, h, i: (b, 0, kb + h)),
                  pl.BlockSpec((None, seq, w), lambda b, h, i: (b, 0, vb + h)),
                  pl.BlockSpec((None, tq, w), lambda b, h, i: (b, i, gb + h)),
                  pl.BlockSpec((1, w), lambda b, h, i: (0, h))],
        out_specs=pl.BlockSpec((None, tq, w), lambda b, h, i: (b, i, h)),
        out_shape=jax.ShapeDtypeStruct((batch, seq, SB_HEADS * HEAD_DIM), BF16),
        compiler_params=pltpu.CompilerParams(
            dimension_semantics=("arbitrary", "arbitrary", "arbitrary"),
            vmem_limit_bytes=VMEM_LIMIT_BYTES),
        name="sb_attention",
    )(proj, proj, proj, proj, gain)


def _dsa_kernel(iq_lo_ref, iq_hi_ref, ik_ref, iw_ref, q_ref, k_ref, vt_ref, gate_lo_ref, gate_hi_ref, gain_ref,
                o_ref, sc_scr, qs_scr, s_scr, p_scr, *, tq, ck, topk, seq):
    i = pl.program_id(1)
    n_chunks = (i * tq) // ck + 1
    rep = DSA_HEADS // DSA_KV_HEADS
    kf = float(topk)
    groups = ck // SUBLANES

    iw_t = iw_ref[...].T
    q_pos = i * tq + lax.broadcasted_iota(I32, (ck, tq), 1)
    key_off = lax.broadcasted_iota(I32, (ck, tq), 0)

    def score_chunk(c, carry):
        mn, mx = carry
        start = pl.multiple_of(c * ck, ck)
        k_even = ik_ref[pl.ds(start, ck), 0:LANES]
        k_odd = ik_ref[pl.ds(start, ck), LANES:2 * LANES]
        acc = jnp.zeros((ck, tq), F32)
        pairs = IDX_HEADS // 2
        for p in range(pairs):
            half_ref, pp = (iq_lo_ref, p) if p < pairs // 2 else (iq_hi_ref, p - pairs // 2)
            qp = half_ref[:, pp * LANES:(pp + 1) * LANES]
            for hh, kk in ((2 * p, k_even), (2 * p + 1, k_odd)):
                acc = acc + iw_t[hh:hh + 1, :] * jnp.maximum(_dot_nt(kk, qp), 0.0)
        causal = (c * ck + key_off) <= q_pos
        sc_scr[c] = jnp.where(causal, acc, -jnp.inf)
        mx = jnp.maximum(mx, jnp.max(jnp.where(causal, acc, -jnp.inf), axis=0, keepdims=True))
        mn = jnp.minimum(mn, jnp.min(jnp.where(causal, acc, jnp.inf), axis=0, keepdims=True))
        return mn, mx

    row_min, row_max = lax.fori_loop(0, n_chunks, score_chunk,
                                     (jnp.full((1, tq), jnp.inf, F32), jnp.full((1, tq), -jnp.inf, F32)))

    def rows8(v):
        return jnp.broadcast_to(v, (SUBLANES, tq))

    def count_where(pred):
        def body(c, cnt):
            hit = jnp.where(pred(sc_scr[c].reshape(groups, SUBLANES, tq), c), 1.0, 0.0)
            return cnt + jnp.sum(hit.reshape(groups // COUNT_LANES, COUNT_LANES, SUBLANES, tq), axis=0)
        cnt = lax.fori_loop(0, n_chunks, body, jnp.zeros((COUNT_LANES, SUBLANES, tq), F32))
        return jnp.sum(jnp.sum(cnt, axis=0), axis=0, keepdims=True)

    def count_ge(mid):
        mid8 = rows8(mid)

        def body(c, cnt):
            cnt = list(cnt)
            for j in range(groups):
                blk = sc_scr[c, j * SUBLANES:(j + 1) * SUBLANES, :]
                cnt[j % COUNT_LANES] = cnt[j % COUNT_LANES] + jnp.where(blk >= mid8, 1.0, 0.0)
            return tuple(cnt)

        cnt = lax.fori_loop(0, n_chunks, body, tuple(jnp.zeros((SUBLANES, tq), F32) for _ in range(COUNT_LANES)))
        return jnp.sum(functools.reduce(lambda a, b: a + b, cnt), axis=0, keepdims=True)

    n_valid = (i * tq + lax.broadcasted_iota(I32, (1, tq), 1) + 1).astype(F32)
    few = n_valid <= kf

    def bisect_pass(state):
        lo, hi, cnt_lo, done = state
        mid = 0.5 * lo + 0.5 * hi
        no_gap = jnp.logical_or(mid <= lo, mid >= hi)
        cnt = count_ge(mid)
        ge = cnt >= kf
        live = done < 0.5
        up = jnp.logical_and(live, ge)
        down = jnp.logical_and(live, jnp.logical_not(ge))
        lo = jnp.where(up, mid, lo)
        cnt_lo = jnp.where(up, cnt, cnt_lo)
        hi = jnp.where(down, mid, hi)
        done = jnp.where(jnp.logical_or(cnt_lo == kf, no_gap), 1.0, done)
        return lo, hi, cnt_lo, done

    def search_cond(carry):
        it, all_done = carry[0], carry[1]
        return jnp.logical_and(it < SEARCH_PASS_CAP, all_done < 0.5)

    def search_body(carry):
        state = carry[2:]
        for _ in range(SEARCH_UNROLL):
            state = bisect_pass(state)
        return (carry[0] + SEARCH_UNROLL, jnp.min(state[3])) + tuple(state)

    hi0 = row_max + (row_max - row_min) * (2.0 ** -10) + 1e-30
    done0 = jnp.where(few, 1.0, 0.0)
    carry = lax.while_loop(search_cond, search_body,
                           (jnp.int32(0), jnp.min(done0), row_min, hi0, n_valid, done0))
    thr, cnt_thr = carry[2], carry[4]
    thr8 = rows8(thr)

    excess = jnp.logical_and(cnt_thr > kf, jnp.logical_not(few))

    @pl.when(jnp.max(jnp.where(excess, 1.0, 0.0)) > 0.5)
    def _():
        need = kf - count_where(lambda s, c: s > thr8)
        key_in_chunk = lax.broadcasted_iota(I32, (groups, SUBLANES, tq), 0) * SUBLANES + lax.broadcasted_iota(
            I32, (groups, SUBLANES, tq), 1)

        def key_f(c):
            return (c * ck + key_in_chunk).astype(F32)

        def pos_pass(b, last_short):
            cand = last_short + jnp.left_shift(jnp.int32(1), b).astype(F32)
            cand8 = rows8(cand)
            cnt = count_where(lambda s, c: jnp.logical_and(s == thr8, key_f(c) <= cand8))
            return jnp.where(cnt < need, cand, last_short)

        n_bits = (seq - 1).bit_length()
        last_short = lax.fori_loop(0, n_bits, lambda t, x: pos_pass(n_bits - 1 - t, x),
                                   jnp.full((1, tq), -1.0, F32))
        keep_upto = rows8(jnp.where(excess, last_short + 1.0, float(seq)))

        def drop(c, carry):
            s = sc_scr[c].reshape(groups, SUBLANES, tq)
            cut = jnp.logical_and(s == thr8, key_f(c) > keep_upto)
            sc_scr[c] = jnp.where(cut, -jnp.inf, s).reshape(ck, tq)
            return carry

        lax.fori_loop(0, n_chunks, drop, 0)

    for g in range(DSA_KV_HEADS):
        for r in range(rep):
            h = g * rep + r
            qs_scr[g, r * tq:(r + 1) * tq, :] = q_ref[:, h * HEAD_DIM:(h + 1) * HEAD_DIM]
    def attend_chunk(c, stats):
        start = pl.multiple_of(c * ck, ck)
        bias = jnp.where(sc_scr[c] >= thr, 0.0, MASK_BIAS)
        bias = jnp.concatenate([bias] * rep, axis=1)
        kv = range(DSA_KV_HEADS)
        m_news = []
        for g in kv:
            kg = k_ref[pl.ds(start, ck), g * HEAD_DIM:(g + 1) * HEAD_DIM]
            s = _dot_nt(kg, qs_scr[g]) + bias
            s_scr[g] = s
            m_news.append(jnp.maximum(stats[g][0], jnp.max(s, axis=0, keepdims=True)))
        for g in kv:
            p_scr[g] = jnp.exp2(s_scr[g] - m_news[g]).astype(BF16)
        out = []
        for g in kv:
            m_old, l_old, acc_old = stats[g]
            alpha = jnp.exp2(m_old - m_news[g])
            pv = _dot(vt_ref[c, g * VT_ROWS:(g + 1) * VT_ROWS, :], p_scr[g])
            out.append((m_news[g], alpha * l_old + pv[HEAD_DIM:HEAD_DIM + 1],
                        alpha * acc_old + pv[:HEAD_DIM]))
        return tuple(out)

    init = tuple((jnp.full((1, rep * tq), MASK_BIAS, F32), jnp.zeros((1, rep * tq), F32),
                  jnp.zeros((HEAD_DIM, rep * tq), F32)) for _ in range(DSA_KV_HEADS))
    fin = lax.fori_loop(0, n_chunks, attend_chunk, init)

    for g in range(DSA_KV_HEADS):
        o_t = fin[g][2] * (1.0 / fin[g][1])
        y_t = _unit_rms(o_t, 0)
        gate_ref = (gate_lo_ref, gate_hi_ref)[g]
        for r in range(rep):
            sl = slice((g * rep + r) * HEAD_DIM, (g * rep + r + 1) * HEAD_DIM)
            gsl = slice(r * HEAD_DIM, (r + 1) * HEAD_DIM)
            y = y_t[:, r * tq:(r + 1) * tq].T
            o_ref[:, sl] = _gain_gate(y, gain_ref[:, sl], gate_ref[:, gsl]).astype(o_ref.dtype)


def _dsa_attention(proj, ikp, iw, vt, gain, *, batch, seq, iq_col, q_col, k_col, gate_col):
    tq, ck = DSA_QUERIES, DSA_KEYS
    assert seq % ck == 0 and seq % tq == 0
    topk = min(TOPK_MAX, seq // 4)
    dq = DSA_HEADS * HEAD_DIM
    dkv = DSA_KV_HEADS * HEAD_DIM
    rep = DSA_HEADS // DSA_KV_HEADS
    half = dq // 2
    assert IDX_HEADS * IDX_DIM == dq and all(c % half == 0 for c in (iq_col, gate_col))
    assert q_col % dq == 0 and k_col % dkv == 0

    def half_block(col, which):
        return pl.BlockSpec((None, tq, half), lambda b, i: (b, i, col // half + which))

    return pl.pallas_call(
        functools.partial(_dsa_kernel, tq=tq, ck=ck, topk=topk, seq=seq),
        grid=(batch, seq // tq),
        in_specs=[half_block(iq_col, 0), half_block(iq_col, 1),
                  pl.BlockSpec((None, seq, 2 * LANES), lambda b, i: (b, 0, 0)),
                  pl.BlockSpec((None, tq, LANES), lambda b, i: (b, i, 0)),
                  pl.BlockSpec((None, tq, dq), lambda b, i: (b, i, q_col // dq)),
                  pl.BlockSpec((None, seq, dkv), lambda b, i: (b, 0, k_col // dkv)),
                  pl.BlockSpec((None, seq // ck, DSA_KV_HEADS * VT_ROWS, ck), lambda b, i: (b, 0, 0, 0)),
                  half_block(gate_col, 0), half_block(gate_col, 1),
                  pl.BlockSpec((1, dq), lambda b, i: (0, 0))],
        out_specs=pl.BlockSpec((None, tq, dq), lambda b, i: (b, i, 0)),
        out_shape=jax.ShapeDtypeStruct((batch, seq, dq), BF16),
        scratch_shapes=[pltpu.VMEM((seq // ck, ck, tq), F32),
                        pltpu.VMEM((DSA_KV_HEADS, rep * tq, HEAD_DIM), BF16),
                        pltpu.VMEM((DSA_KV_HEADS, ck, rep * tq), F32),
                        pltpu.VMEM((DSA_KV_HEADS, ck, rep * tq), BF16)],
        compiler_params=pltpu.CompilerParams(
            dimension_semantics=("arbitrary", "arbitrary"), vmem_limit_bytes=VMEM_LIMIT_BYTES),
        name="dsa_attention",
    )(proj, proj, ikp, iw, proj, proj, vt, proj, proj, gain)


def _out_kernel(a_ref, b_ref, wa_ref, wb_ref, x_ref, g_ref, o_ref):
    y = x_ref[...] + _dot(a_ref[...], wa_ref[...]) + _dot(b_ref[...], wb_ref[...])
    ms = jnp.mean(y * y, axis=-1, keepdims=True)
    o_ref[...] = y * lax.rsqrt(ms + EPS) * g_ref[...]


def _out_proj(mix_a, mix_b, w_a, w_b, x2, final_g):
    m, d = x2.shape
    tm = OUT_ROWS
    ka, kb = mix_a.shape[1], mix_b.shape[1]
    return pl.pallas_call(
        _out_kernel,
        grid=(m // tm,),
        in_specs=[pl.BlockSpec((tm, ka), lambda i: (i, 0)),
                  pl.BlockSpec((tm, kb), lambda i: (i, 0)),
                  pl.BlockSpec((ka, d), lambda i: (0, 0)),
                  pl.BlockSpec((kb, d), lambda i: (0, 0)),
                  pl.BlockSpec((tm, d), lambda i: (i, 0)),
                  pl.BlockSpec((1, d), lambda i: (0, 0))],
        out_specs=pl.BlockSpec((tm, d), lambda i: (i, 0)),
        out_shape=jax.ShapeDtypeStruct((m, d), F32),
        compiler_params=pltpu.CompilerParams(
            dimension_semantics=("arbitrary",), vmem_limit_bytes=VMEM_LIMIT_BYTES),
        name="out_proj",
    )(mix_a, mix_b, w_a, w_b, x2, final_g)


def kernel(x, positions, norm_g, w_in, sb_out_g, dsa_out_g, w_out, final_g):
    batch, seq, d_model = x.shape
    d_sb = SB_HEADS * HEAD_DIM
    d_dsa = DSA_HEADS * HEAD_DIM
    d_kv = DSA_KV_HEADS * HEAD_DIM
    d_iq = IDX_HEADS * IDX_DIM
    depth = norm_g.shape[0]
    assert depth == 1 and d_sb + d_dsa == w_out.shape[1]
    m = batch * seq
    x2 = x.reshape(m, d_model)
    layer = 0

    names = ("sb_q", "sb_k", "sb_v", "sb_gate", "dsa_q", "dsa_k", "dsa_v", "dsa_gate", "idx_q", "idx_k", "idx_w")
    sizes = (d_sb, d_sb, d_sb, d_sb, d_dsa, d_kv, d_kv, d_dsa, d_iq, IDX_DIM, IDX_HEADS)
    col, off = {}, 0
    for name, size in zip(names, sizes):
        col[name] = off
        off += size
    n_main = col["idx_k"]
    assert n_main % PROJ_COLS == 0

    wl = w_in[layer]
    w_bf = wl[:, :n_main].astype(BF16)
    w_ik = wl[:, col["idx_k"]:col["idx_k"] + IDX_DIM].astype(BF16)
    zeros64 = jnp.zeros((d_model, LANES - IDX_DIM), BF16)
    w_tail = jnp.concatenate([w_ik, zeros64, zeros64, w_ik,
                              wl[:, col["idx_w"]:col["idx_w"] + IDX_HEADS].astype(BF16),
                              jnp.zeros((d_model, LANES - IDX_HEADS), BF16)], axis=1)
    w_vt = wl[:, col["dsa_v"]:col["dsa_v"] + d_kv].T.astype(BF16)

    att_scale = HEAD_DIM ** -0.5
    chunk = jnp.arange(n_main) // LANES
    colscale = jnp.where(chunk < d_sb // LANES, att_scale, 1.0)
    colscale = jnp.where((chunk >= col["dsa_q"] // LANES) & (chunk < col["dsa_k"] // LANES),
                         att_scale * LOG2_E, colscale).astype(F32).reshape(1, n_main)
    kinds = ["plain"] * (n_main // LANES)
    for j in range(col["dsa_q"] // LANES, col["dsa_v"] // LANES):
        kinds[j] = "rope128"
    for j in range(col["idx_q"] // LANES, col["idx_k"] // LANES):
        kinds[j] = "rope64"

    tables = _rope_tables(positions)
    proj, ikp, iw, vt = _proj(x2, norm_g[layer].reshape(1, d_model), w_bf, colscale, tables, w_tail, w_vt,
                              n_cols=n_main, tn=PROJ_COLS, kinds=tuple(kinds),
                              iw_scale=(IDX_HEADS ** -0.5) * (IDX_DIM ** -0.5), t_chunk=DSA_KEYS)

    proj3 = proj.reshape(batch, seq, n_main)
    mix_sb = _sb_attention(proj3, sb_out_g[layer].reshape(1, d_sb), batch=batch, seq=seq,
                           q_col=col["sb_q"], k_col=col["sb_k"], v_col=col["sb_v"], gate_col=col["sb_gate"])
    vt4 = vt.reshape(batch, seq // DSA_KEYS, DSA_KV_HEADS * VT_ROWS, DSA_KEYS)
    mix_dsa = _dsa_attention(proj3, ikp.reshape(batch, seq, -1), iw.reshape(batch, seq, -1), vt4,
                             dsa_out_g[layer].reshape(1, d_dsa), batch=batch, seq=seq,
                             iq_col=col["idx_q"], q_col=col["dsa_q"], k_col=col["dsa_k"], gate_col=col["dsa_gate"])

    w_o = w_out[layer].astype(BF16)
    out = _out_proj(mix_sb.reshape(m, d_sb), mix_dsa.reshape(m, d_dsa), w_o[:d_sb], w_o[d_sb:], x2,
                    final_g.reshape(1, d_model))
    return out.reshape(batch, seq, d_model)
```

```python
import functools

import jax
import jax.numpy as jnp
import numpy as np
from jax import lax
from jax.experimental import pallas as pl
from jax.experimental.pallas import tpu as pltpu

F32 = jnp.float32
BF16 = jnp.bfloat16
I32 = jnp.int32

HEAD_DIM = 128
SB_HEADS = 8
DSA_HEADS = 8
DSA_KV_HEADS = 2
IDX_HEADS = 16
IDX_DIM = 64
TOPK_MAX = 256
ROPE_THETA = 500000.0
ROPE_FRACTION_DIV = 4
EPS = 1e-6

LANES = 128
SUBLANES = 8
VMEM_LIMIT_BYTES = 56 * 1024 * 1024

PROJ_ROWS = 1024
PROJ_COLS = 1280
OUT_ROWS = 512
SB_BLOCK = 256
SB_HEADS_PER_STEP = 4
DSA_QUERIES = 512
DSA_KEYS = 512
VT_ROWS = HEAD_DIM + 16

MASK_BIAS = -1e30
LOG2_E = 1.4426950408889634
SB_DEAD_LOG = -110.0
SEARCH_PASS_CAP = 320
SEARCH_UNROLL = 4
COUNT_LANES = 8


def _dot_nt(a, b):
    return lax.dot_general(a, b, (((1,), (1,)), ((), ())), preferred_element_type=F32)


def _dot(a, b):
    return jnp.dot(a, b, preferred_element_type=F32)


ROT128 = HEAD_DIM // ROPE_FRACTION_DIV
ROT64 = IDX_DIM // ROPE_FRACTION_DIV


def _fill_rope_tables(pos_ref, pat_ref, c128_ref, s128_ref, c64_ref, s64_ref):
    pos = pos_ref[...].astype(F32)
    ang = pos * pat_ref[0:1, :]
    cb = jnp.cos(ang)
    sb = jnp.sin(ang) * pat_ref[1:2, :]
    lane = lax.broadcasted_iota(I32, ang.shape, 1)
    wide = lane < ROT128
    c128_ref[...] = jnp.where(wide, cb, 1.0)
    s128_ref[...] = jnp.where(wide, sb, 0.0)
    rotated = (lane % IDX_DIM) < ROT64
    low = lane < IDX_DIM
    c64_ref[...] = jnp.where(rotated, jnp.where(low, pltpu.roll(cb, LANES - ROT128, 1),
                                                pltpu.roll(cb, IDX_DIM - ROT128, 1)), 1.0)
    s64_ref[...] = jnp.where(rotated, jnp.where(low, pltpu.roll(sb, LANES - ROT128, 1),
                                                pltpu.roll(sb, IDX_DIM - ROT128, 1)), 0.0)


def _rope_lane_patterns():
    lane = np.arange(LANES)
    in128 = lane < ROT128
    l64 = lane - ROT128
    in64 = (l64 >= 0) & (l64 < ROT64)
    expo = np.where(in128, -(lane % (ROT128 // 2)) / (ROT128 // 2),
                    np.where(in64, -(l64 % (ROT64 // 2)) / (ROT64 // 2), 0.0)).astype(np.float32)
    sign = np.where(in128, np.where(lane < ROT128 // 2, -1.0, 1.0),
                    np.where(in64, np.where(l64 < ROT64 // 2, -1.0, 1.0), 0.0)).astype(np.float32)
    invf = jnp.where(jnp.asarray(in128 | in64), jnp.power(ROPE_THETA, jnp.asarray(expo)), 0.0)
    return jnp.stack([invf.astype(F32), jnp.asarray(sign)])


def _apply_rope(x, cos, sin, kind):
    lane = lax.broadcasted_iota(I32, x.shape, 1)
    if kind == "rope128":
        half = HEAD_DIM // ROPE_FRACTION_DIV // 2
        first = lane < half
    else:
        half = IDX_DIM // ROPE_FRACTION_DIV // 2
        first = (lane % IDX_DIM) < half
    partner = jnp.where(first, pltpu.roll(x, LANES - half, 1), pltpu.roll(x, half, 1))
    return x * cos + partner * sin


def _proj_kernel(x_ref, g_ref, w_ref, cs_ref, pos_ref, pat_ref, wtail_ref,
                 o_ref, ik_ref, iw_ref, vt_ref, h_scr, c128_ref, s128_ref, c64_ref, s64_ref, *, tile_kinds, iw_scale):
    n = pl.program_id(1)

    def finish(xj, kind):
        if kind == "rope128":
            return _apply_rope(xj, c128_ref[...], s128_ref[...], "rope128")
        if kind == "rope64":
            return _apply_rope(xj, c64_ref[...], s64_ref[...], "rope64")
        return xj

    @pl.when(n == 0)
    def _():
        _fill_rope_tables(pos_ref, pat_ref, c128_ref, s128_ref, c64_ref, s64_ref)
        xf = x_ref[...]
        ms = jnp.mean(xf * xf, axis=-1, keepdims=True)
        h_scr[...] = (xf * lax.rsqrt(ms + EPS) * g_ref[...]).astype(BF16)
        tail = _dot(h_scr[...], wtail_ref[...])
        for j in range(2):
            ik_ref[:, j * LANES:(j + 1) * LANES] = finish(
                tail[:, j * LANES:(j + 1) * LANES], "rope64").astype(ik_ref.dtype)
        iw_ref[...] = tail[:, 2 * LANES:3 * LANES] * iw_scale

    for pattern, tiles in tile_kinds:
        @pl.when(functools.reduce(jnp.logical_or, [n == t for t in tiles]))
        def _(pattern=pattern):
            acc = _dot(h_scr[...], w_ref[...]) * cs_ref[...]
            for j, kind in enumerate(pattern):
                o_ref[:, j * LANES:(j + 1) * LANES] = finish(
                    acc[:, j * LANES:(j + 1) * LANES], kind).astype(o_ref.dtype)
            heads = [j for j, kind in enumerate(pattern) if kind == "value"]
            if heads:
                cw = vt_ref.shape[2]
                for g, j in enumerate(heads):
                    v_t = acc[:, j * LANES:(j + 1) * LANES].T.astype(vt_ref.dtype)
                    for c in range(vt_ref.shape[0]):
                        vt_ref[c, g * VT_ROWS:g * VT_ROWS + HEAD_DIM, :] = v_t[:, c * cw:(c + 1) * cw]
                        vt_ref[c, g * VT_ROWS + HEAD_DIM:(g + 1) * VT_ROWS, :] = jnp.ones(
                            (VT_ROWS - HEAD_DIM, cw), vt_ref.dtype)


def _proj(x2, g, w, colscale, pos, w_tail, *, n_cols, tn, kinds, iw_scale, t_chunk):
    m, d = x2.shape
    tm = PROJ_ROWS
    per_tile = tn // LANES
    n_tiles = n_cols // tn
    patterns = {}
    for t in range(n_tiles):
        patterns.setdefault(tuple(kinds[t * per_tile:(t + 1) * per_tile]), []).append(t)
    assert sum("value" in p for p in patterns) == 1 and all(len(ts) == 1 for p, ts in patterns.items() if "value" in p)
    t_rows = kinds.count("value") * VT_ROWS
    row_tile = lambda i, j: (i, 0)
    whole = lambda i, j: (0, 0)
    return pl.pallas_call(
        functools.partial(_proj_kernel, tile_kinds=tuple(patterns.items()), iw_scale=iw_scale),
        grid=(m // tm, n_tiles),
        in_specs=[pl.BlockSpec((tm, d), row_tile),
                  pl.BlockSpec((1, d), whole),
                  pl.BlockSpec((d, tn), lambda i, j: (0, j)),
                  pl.BlockSpec((1, tn), lambda i, j: (0, j)),
                  pl.BlockSpec((tm, 1), row_tile),
                  pl.BlockSpec((2, LANES), whole),
                  pl.BlockSpec(w_tail.shape, whole)],
        out_specs=[pl.BlockSpec((tm, tn), lambda i, j: (i, j)),
                   pl.BlockSpec((tm, 2 * LANES), row_tile),
                   pl.BlockSpec((tm, LANES), row_tile),
                   pl.BlockSpec((tm // t_chunk, t_rows, t_chunk), lambda i, j: (i, 0, 0))],
        out_shape=[jax.ShapeDtypeStruct((m, n_cols), BF16),
                   jax.ShapeDtypeStruct((m, 2 * LANES), BF16),
                   jax.ShapeDtypeStruct((m, LANES), F32),
                   jax.ShapeDtypeStruct((m // t_chunk, t_rows, t_chunk), BF16)],
        scratch_shapes=[pltpu.VMEM((tm, d), BF16)] + [pltpu.VMEM((tm, LANES), F32)] * 4,
        compiler_params=pltpu.CompilerParams(
            dimension_semantics=("arbitrary", "arbitrary"), vmem_limit_bytes=VMEM_LIMIT_BYTES),
        name="in_proj",
    )(x2, g, w, colscale, pos, _rope_lane_patterns(), w_tail)


def _unit_rms(o, axis):
    return o * lax.rsqrt(jnp.mean(o * o, axis=axis, keepdims=True) + EPS)


def _gain_gate(y, gain, gate):
    gf = gate.astype(F32)
    return y * gain * (gf * (1.0 / (1.0 + jnp.exp(-gf))))


def _norm_gate(o, gain, gate):
    return _gain_gate(_unit_rms(o, -1), gain, gate)


def _sb_kernel(q_ref, k_ref, v_ref, gate_ref, gain_ref, o_ref, *, tq, hps):
    i = pl.program_id(2)
    row = lax.broadcasted_iota(I32, (tq, tq), 0)
    col = lax.broadcasted_iota(I32, (tq, tq), 1)
    later = jnp.where(row > col, 1.0, 0.0).astype(BF16)
    strict = col < row
    qs = [q_ref[:, h * HEAD_DIM:(h + 1) * HEAD_DIM] for h in range(hps)]

    def block(kb, cs, accs, diag):
        start = pl.multiple_of(kb * tq, tq)
        heads = range(hps)
        zs = [_dot_nt(qs[h], k_ref[pl.ds(start, tq), h * HEAD_DIM:(h + 1) * HEAD_DIM]) for h in heads]
        log_betas, log_1ms, his, los = [], [], [], []
        for h in heads:
            z = zs[h]
            sp = jnp.log(1.0 + jnp.exp(-jnp.abs(z)))
            log_beta = jnp.minimum(z, 0.0) - sp
            log_1m = log_beta - z
            if diag:
                log_beta = jnp.where(strict, log_beta, -jnp.inf)
                log_1m = jnp.where(strict, log_1m, 0.0)
            hi = log_1m.astype(BF16)
            log_betas.append(log_beta)
            log_1ms.append(log_1m)
            his.append(hi)
            los.append((log_1m - hi.astype(F32)).astype(BF16))
        tails = [_dot(his[h], later) + _dot(los[h], later) for h in heads]
        ws = [jnp.exp(log_betas[h] + tails[h] + cs[h]).astype(BF16) for h in heads]
        new_accs = [accs[h] + _dot(ws[h], v_ref[pl.ds(start, tq), h * HEAD_DIM:(h + 1) * HEAD_DIM])
                    for h in heads]
        new_cs = [cs[h] + jnp.sum(log_1ms[h], axis=1, keepdims=True) for h in heads]
        return new_cs, new_accs

    def dead_bound(cs):
        return functools.reduce(jnp.maximum, [jnp.max(c) for c in cs])

    cs0 = [jnp.zeros((tq, 1), F32) for _ in range(hps)]
    accs0 = [jnp.zeros((tq, HEAD_DIM), F32) for _ in range(hps)]
    cs1, accs1 = block(i, cs0, accs0, True)

    def cond(carry):
        kb, bound, _, _ = carry
        return jnp.logical_and(kb >= 0, bound > SB_DEAD_LOG)

    def body(carry):
        kb, _, cs, accs = carry
        cs, accs = block(kb, list(cs), list(accs), False)
        return kb - 1, dead_bound(cs), tuple(cs), tuple(accs)

    _, _, _, accs = lax.while_loop(cond, body, (i - 1, dead_bound(cs1), tuple(cs1), tuple(accs1)))

    for h in range(hps):
        sl = slice(h * HEAD_DIM, (h + 1) * HEAD_DIM)
        o_ref[:, sl] = _norm_gate(accs[h], gain_ref[:, sl], gate_ref[:, sl]).astype(o_ref.dtype)


def _sb_attention(proj, gain, *, batch, seq, q_col, k_col, v_col, gate_col):
    tq, hps = SB_BLOCK, SB_HEADS_PER_STEP
    w = hps * HEAD_DIM
    qb, kb, vb, gb = (c // w for c in (q_col, k_col, v_col, gate_col))
    return pl.pallas_call(
        functools.partial(_sb_kernel, tq=tq, hps=hps),
        grid=(batch, SB_HEADS // hps, seq // tq),
        in_specs=[pl.BlockSpec((None, tq, w), lambda b, h, i: (b, i, qb + h)),
                  pl.BlockSpec((None, seq, w), lambda b, h, i: (b, 0, kb + h)),
                  pl.BlockSpec((None, seq, w), lambda b, h, i: (b, 0, vb + h)),
                  pl.BlockSpec((None, tq, w), lambda b, h, i: (b, i, gb + h)),
                  pl.BlockSpec((1, w), lambda b, h, i: (0, h))],
        out_specs=pl.BlockSpec((None, tq, w), lambda b, h, i: (b, i, h)),
        out_shape=jax.ShapeDtypeStruct((batch, seq, SB_HEADS * HEAD_DIM), BF16),
        compiler_params=pltpu.CompilerParams(
            dimension_semantics=("arbitrary", "arbitrary", "arbitrary"),
            vmem_limit_bytes=VMEM_LIMIT_BYTES),
        name="sb_attention",
    )(proj, proj, proj, proj, gain)


def _dsa_kernel(iq_lo_ref, iq_hi_ref, ik_ref, iw_ref, q_ref, k_ref, vt_ref, gate_lo_ref, gate_hi_ref, gain_ref,
                o_ref, sc_scr, qs_scr, s_scr, p_scr, *, tq, ck, topk, seq):
    i = pl.program_id(1)
    n_chunks = (i * tq) // ck + 1
    rep = DSA_HEADS // DSA_KV_HEADS
    kf = float(topk)
    groups = ck // SUBLANES

    iw_t = iw_ref[...].T
    q_pos = i * tq + lax.broadcasted_iota(I32, (ck, tq), 1)
    key_off = lax.broadcasted_iota(I32, (ck, tq), 0)

    def score_chunk(c, carry):
        mn, mx = carry
        start = pl.multiple_of(c * ck, ck)
        k_even = ik_ref[pl.ds(start, ck), 0:LANES]
        k_odd = ik_ref[pl.ds(start, ck), LANES:2 * LANES]
        acc = jnp.zeros((ck, tq), F32)
        pairs = IDX_HEADS // 2
        for p in range(pairs):
            half_ref, pp = (iq_lo_ref, p) if p < pairs // 2 else (iq_hi_ref, p - pairs // 2)
            qp = half_ref[:, pp * LANES:(pp + 1) * LANES]
            for hh, kk in ((2 * p, k_even), (2 * p + 1, k_odd)):
                acc = acc + iw_t[hh:hh + 1, :] * jnp.maximum(_dot_nt(kk, qp), 0.0)
        causal = (c * ck + key_off) <= q_pos
        sc_scr[c] = jnp.where(causal, acc, -jnp.inf)
        mx = jnp.maximum(mx, jnp.max(jnp.where(causal, acc, -jnp.inf), axis=0, keepdims=True))
        mn = jnp.minimum(mn, jnp.min(jnp.where(causal, acc, jnp.inf), axis=0, keepdims=True))
        return mn, mx

    row_min, row_max = lax.fori_loop(0, n_chunks, score_chunk,
                                     (jnp.full((1, tq), jnp.inf, F32), jnp.full((1, tq), -jnp.inf, F32)))

    def rows8(v):
        return jnp.broadcast_to(v, (SUBLANES, tq))

    def count_where(pred):
        def body(c, cnt):
            hit = jnp.where(pred(sc_scr[c].reshape(groups, SUBLANES, tq), c), 1.0, 0.0)
            return cnt + jnp.sum(hit.reshape(groups // COUNT_LANES, COUNT_LANES, SUBLANES, tq), axis=0)
        cnt = lax.fori_loop(0, n_chunks, body, jnp.zeros((COUNT_LANES, SUBLANES, tq), F32))
        return jnp.sum(jnp.sum(cnt, axis=0), axis=0, keepdims=True)

    def count_ge(mid):
        mid8 = rows8(mid)

        def body(c, cnt):
            cnt = list(cnt)
            for j in range(groups):
                blk = sc_scr[c, j * SUBLANES:(j + 1) * SUBLANES, :]
                cnt[j % COUNT_LANES] = cnt[j % COUNT_LANES] + jnp.where(blk >= mid8, 1.0, 0.0)
            return tuple(cnt)

        cnt = lax.fori_loop(0, n_chunks, body, tuple(jnp.zeros((SUBLANES, tq), F32) for _ in range(COUNT_LANES)))
        return jnp.sum(functools.reduce(lambda a, b: a + b, cnt), axis=0, keepdims=True)

    n_valid = (i * tq + lax.broadcasted_iota(I32, (1, tq), 1) + 1).astype(F32)
    few = n_valid <= kf

    def bisect_pass(state):
        lo, hi, cnt_lo, done = state
        mid = 0.5 * lo + 0.5 * hi
        no_gap = jnp.logical_or(mid <= lo, mid >= hi)
        cnt = count_ge(mid)
        ge = cnt >= kf
        live = done < 0.5
        up = jnp.logical_and(live, ge)
        down = jnp.logical_and(live, jnp.logical_not(ge))
        lo = jnp.where(up, mid, lo)
        cnt_lo = jnp.where(up, cnt, cnt_lo)
        hi = jnp.where(down, mid, hi)
        done = jnp.where(jnp.logical_or(cnt_lo == kf, no_gap), 1.0, done)
        return lo, hi, cnt_lo, done

    def search_cond(carry):
        it, all_done = carry[0], carry[1]
        return jnp.logical_and(it < SEARCH_PASS_CAP, all_done < 0.5)

    def search_body(carry):
        state = carry[2:]
        for _ in range(SEARCH_UNROLL):
            state = bisect_pass(state)
        return (carry[0] + SEARCH_UNROLL, jnp.min(state[3])) + tuple(state)

    hi0 = row_max + (row_max - row_min) * (2.0 ** -10) + 1e-30
    done0 = jnp.where(few, 1.0, 0.0)
    carry = lax.while_loop(search_cond, search_body,
                           (jnp.int32(0), jnp.min(done0), row_min, hi0, n_valid, done0))
    thr, cnt_thr = carry[2], carry[4]
    thr8 = rows8(thr)

    excess = jnp.logical_and(cnt_thr > kf, jnp.logical_not(few))

    @pl.when(jnp.max(jnp.where(excess, 1.0, 0.0)) > 0.5)
    def _():
        need = kf - count_where(lambda s, c: s > thr8)
        key_in_chunk = lax.broadcasted_iota(I32, (groups, SUBLANES, tq), 0) * SUBLANES + lax.broadcasted_iota(
            I32, (groups, SUBLANES, tq), 1)

        def key_f(c):
            return (c * ck + key_in_chunk).astype(F32)

        def pos_pass(b, last_short):
            cand = last_short + jnp.left_shift(jnp.int32(1), b).astype(F32)
            cand8 = rows8(cand)
            cnt = count_where(lambda s, c: jnp.logical_and(s == thr8, key_f(c) <= cand8))
            return jnp.where(cnt < need, cand, last_short)

        n_bits = (seq - 1).bit_length()
        last_short = lax.fori_loop(0, n_bits, lambda t, x: pos_pass(n_bits - 1 - t, x),
                                   jnp.full((1, tq), -1.0, F32))
        keep_upto = rows8(jnp.where(excess, last_short + 1.0, float(seq)))

        def drop(c, carry):
            s = sc_scr[c].reshape(groups, SUBLANES, tq)
            cut = jnp.logical_and(s == thr8, key_f(c) > keep_upto)
            sc_scr[c] = jnp.where(cut, -jnp.inf, s).reshape(ck, tq)
            return carry

        lax.fori_loop(0, n_chunks, drop, 0)

    for g in range(DSA_KV_HEADS):
        for r in range(rep):
            h = g * rep + r
            qs_scr[g, r * tq:(r + 1) * tq, :] = q_ref[:, h * HEAD_DIM:(h + 1) * HEAD_DIM]
    def attend_chunk(c, stats):
        start = pl.multiple_of(c * ck, ck)
        bias = jnp.where(sc_scr[c] >= thr, 0.0, MASK_BIAS)
        bias = jnp.concatenate([bias] * rep, axis=1)
        kv = range(DSA_KV_HEADS)
        m_news = []
        for g in kv:
            kg = k_ref[pl.ds(start, ck), g * HEAD_DIM:(g + 1) * HEAD_DIM]
            s = _dot_nt(kg, qs_scr[g]) + bias
            s_scr[g] = s
            m_news.append(jnp.maximum(stats[g][0], jnp.max(s, axis=0, keepdims=True)))
        for g in kv:
            p_scr[g] = jnp.exp2(s_scr[g] - m_news[g]).astype(BF16)
        out = []
        for g in kv:
            m_old, l_old, acc_old = stats[g]
            alpha = jnp.exp2(m_old - m_news[g])
            pv = _dot(vt_ref[c, g * VT_ROWS:(g + 1) * VT_ROWS, :], p_scr[g])
            out.append((m_news[g], alpha * l_old + pv[HEAD_DIM:HEAD_DIM + 1],
                        alpha * acc_old + pv[:HEAD_DIM]))
        return tuple(out)

    init = tuple((jnp.full((1, rep * tq), MASK_BIAS, F32), jnp.zeros((1, rep * tq), F32),
                  jnp.zeros((HEAD_DIM, rep * tq), F32)) for _ in range(DSA_KV_HEADS))
    fin = lax.fori_loop(0, n_chunks, attend_chunk, init)

    for g in range(DSA_KV_HEADS):
        o_t = fin[g][2] * (1.0 / fin[g][1])
        y_t = _unit_rms(o_t, 0)
        gate_ref = (gate_lo_ref, gate_hi_ref)[g]
        for r in range(rep):
            sl = slice((g * rep + r) * HEAD_DIM, (g * rep + r + 1) * HEAD_DIM)
            gsl = slice(r * HEAD_DIM, (r + 1) * HEAD_DIM)
            y = y_t[:, r * tq:(r + 1) * tq].T
            o_ref[:, sl] = _gain_gate(y, gain_ref[:, sl], gate_ref[:, gsl]).astype(o_ref.dtype)


def _dsa_attention(proj, ikp, iw, vt, gain, *, batch, seq, iq_col, q_col, k_col, gate_col):
    tq, ck = DSA_QUERIES, DSA_KEYS
    assert seq % ck == 0 and seq % tq == 0
    topk = min(TOPK_MAX, seq // 4)
    dq = DSA_HEADS * HEAD_DIM
    dkv = DSA_KV_HEADS * HEAD_DIM
    rep = DSA_HEADS // DSA_KV_HEADS
    half = dq // 2
    assert IDX_HEADS * IDX_DIM == dq and all(c % half == 0 for c in (iq_col, gate_col))
    assert q_col % dq == 0 and k_col % dkv == 0

    def half_block(col, which):
        return pl.BlockSpec((None, tq, half), lambda b, i: (b, i, col // half + which))

    return pl.pallas_call(
        functools.partial(_dsa_kernel, tq=tq, ck=ck, topk=topk, seq=seq),
        grid=(batch, seq // tq),
        in_specs=[half_block(iq_col, 0), half_block(iq_col, 1),
                  pl.BlockSpec((None, seq, 2 * LANES), lambda b, i: (b, 0, 0)),
                  pl.BlockSpec((None, tq, LANES), lambda b, i: (b, i, 0)),
                  pl.BlockSpec((None, tq, dq), lambda b, i: (b, i, q_col // dq)),
                  pl.BlockSpec((None, seq, dkv), lambda b, i: (b, 0, k_col // dkv)),
                  pl.BlockSpec((None, seq // ck, DSA_KV_HEADS * VT_ROWS, ck), lambda b, i: (b, 0, 0, 0)),
                  half_block(gate_col, 0), half_block(gate_col, 1),
                  pl.BlockSpec((1, dq), lambda b, i: (0, 0))],
        out_specs=pl.BlockSpec((None, tq, dq), lambda b, i: (b, i, 0)),
        out_shape=jax.ShapeDtypeStruct((batch, seq, dq), BF16),
        scratch_shapes=[pltpu.VMEM((seq // ck, ck, tq), F32),
                        pltpu.VMEM((DSA_KV_HEADS, rep * tq, HEAD_DIM), BF16),
                        pltpu.VMEM((DSA_KV_HEADS, ck, rep * tq), F32),
                        pltpu.VMEM((DSA_KV_HEADS, ck, rep * tq), BF16)],
        compiler_params=pltpu.CompilerParams(
            dimension_semantics=("arbitrary", "arbitrary"), vmem_limit_bytes=VMEM_LIMIT_BYTES),
        name="dsa_attention",
    )(proj, proj, ikp, iw, proj, proj, vt, proj, proj, gain)


def _out_kernel(a_ref, b_ref, wa_ref, wb_ref, x_ref, g_ref, o_ref):
    y = x_ref[...] + _dot(a_ref[...], wa_ref[...]) + _dot(b_ref[...], wb_ref[...])
    ms = jnp.mean(y * y, axis=-1, keepdims=True)
    o_ref[...] = y * lax.rsqrt(ms + EPS) * g_ref[...]


def _out_proj(mix_a, mix_b, w, x2, final_g):
    m, d = x2.shape
    tm = OUT_ROWS
    ka, kb = mix_a.shape[1], mix_b.shape[1]
    assert ka == kb and w.shape[0] == ka + kb
    return pl.pallas_call(
        _out_kernel,
        grid=(m // tm,),
        in_specs=[pl.BlockSpec((tm, ka), lambda i: (i, 0)),
                  pl.BlockSpec((tm, kb), lambda i: (i, 0)),
                  pl.BlockSpec((ka, d), lambda i: (0, 0)),
                  pl.BlockSpec((kb, d), lambda i: (1, 0)),
                  pl.BlockSpec((tm, d), lambda i: (i, 0)),
                  pl.BlockSpec((1, d), lambda i: (0, 0))],
        out_specs=pl.BlockSpec((tm, d), lambda i: (i, 0)),
        out_shape=jax.ShapeDtypeStruct((m, d), F32),
        compiler_params=pltpu.CompilerParams(
            dimension_semantics=("arbitrary",), vmem_limit_bytes=VMEM_LIMIT_BYTES),
        name="out_proj",
    )(mix_a, mix_b, w, w, x2, final_g)


def kernel(x, positions, norm_g, w_in, sb_out_g, dsa_out_g, w_out, final_g):
    batch, seq, d_model = x.shape
    d_sb = SB_HEADS * HEAD_DIM
    d_dsa = DSA_HEADS * HEAD_DIM
    d_kv = DSA_KV_HEADS * HEAD_DIM
    d_iq = IDX_HEADS * IDX_DIM
    depth = norm_g.shape[0]
    assert depth == 1 and d_sb + d_dsa == w_out.shape[1]
    m = batch * seq
    x2 = x.reshape(m, d_model)
    layer = 0

    names = ("sb_q", "sb_k", "sb_v", "sb_gate", "dsa_q", "dsa_k", "dsa_v", "dsa_gate", "idx_q", "idx_k", "idx_w")
    sizes = (d_sb, d_sb, d_sb, d_sb, d_dsa, d_kv, d_kv, d_dsa, d_iq, IDX_DIM, IDX_HEADS)
    col, off = {}, 0
    for name, size in zip(names, sizes):
        col[name] = off
        off += size
    n_main = col["idx_k"]
    assert n_main % PROJ_COLS == 0

    w_bf = w_in[layer].astype(BF16)
    w_ik = w_bf[:, col["idx_k"]:col["idx_k"] + IDX_DIM]
    zeros64 = jnp.zeros((d_model, LANES - IDX_DIM), BF16)
    w_tail = jnp.concatenate([w_ik, zeros64, zeros64, w_ik, w_bf[:, col["idx_w"]:col["idx_w"] + IDX_HEADS],
                              jnp.zeros((d_model, LANES - IDX_HEADS), BF16)], axis=1)

    att_scale = HEAD_DIM ** -0.5
    colscale = np.ones((1, n_main), np.float32)
    colscale[:, col["sb_q"]:col["sb_q"] + d_sb] = att_scale
    colscale[:, col["dsa_q"]:col["dsa_q"] + d_dsa] = att_scale * LOG2_E
    kinds = ["plain"] * (n_main // LANES)
    for j in range(col["dsa_q"] // LANES, col["dsa_v"] // LANES):
        kinds[j] = "rope128"
    for j in range(col["dsa_v"] // LANES, col["dsa_gate"] // LANES):
        kinds[j] = "value"
    for j in range(col["idx_q"] // LANES, col["idx_k"] // LANES):
        kinds[j] = "rope64"

    pos = positions.reshape(m, 1).astype(I32)
    proj, ikp, iw, vt = _proj(x2, norm_g[layer].reshape(1, d_model), w_bf, jnp.asarray(colscale), pos, w_tail,
                              n_cols=n_main, tn=PROJ_COLS, kinds=tuple(kinds),
                              iw_scale=(IDX_HEADS ** -0.5) * (IDX_DIM ** -0.5), t_chunk=DSA_KEYS)

    proj3 = proj.reshape(batch, seq, n_main)
    mix_sb = _sb_attention(proj3, sb_out_g[layer].reshape(1, d_sb), batch=batch, seq=seq,
                           q_col=col["sb_q"], k_col=col["sb_k"], v_col=col["sb_v"], gate_col=col["sb_gate"])
    vt4 = vt.reshape(batch, seq // DSA_KEYS, DSA_KV_HEADS * VT_ROWS, DSA_KEYS)
    mix_dsa = _dsa_attention(proj3, ikp.reshape(batch, seq, -1), iw.reshape(batch, seq, -1), vt4,
                             dsa_out_g[layer].reshape(1, d_dsa), batch=batch, seq=seq,
                             iq_col=col["idx_q"], q_col=col["dsa_q"], k_col=col["dsa_k"], gate_col=col["dsa_gate"])

    w_o = w_out[layer].astype(BF16)
    out = _out_proj(mix_sb.reshape(m, d_sb), mix_dsa.reshape(m, d_dsa), w_o, x2, final_g.reshape(1, d_model))
    return out.reshape(batch, seq, d_model)
```

```python
import functools

import jax
import jax.numpy as jnp
import numpy as np
from jax import lax
from jax.experimental import pallas as pl
from jax.experimental.pallas import tpu as pltpu

F32 = jnp.float32
BF16 = jnp.bfloat16
I32 = jnp.int32

HEAD_DIM = 128
SB_HEADS = 8
DSA_HEADS = 8
DSA_KV_HEADS = 2
IDX_HEADS = 16
IDX_DIM = 64
TOPK_MAX = 256
ROPE_THETA = 500000.0
ROPE_FRACTION_DIV = 4
EPS = 1e-6

LANES = 128
SUBLANES = 8
VMEM_LIMIT_BYTES = 56 * 1024 * 1024

PROJ_ROWS = 1024
PROJ_COLS = 1280
OUT_ROWS = 512
SB_BLOCK = 256
SB_HEADS_PER_STEP = 8
DSA_QUERIES = 512
DSA_KEYS = 512
VT_ROWS = HEAD_DIM + 16

MASK_BIAS = -1e30
LOG2_E = 1.4426950408889634
SB_DEAD_LOG = -110.0
SEARCH_PASS_CAP = 320
SEARCH_UNROLL = 4
COUNT_LANES = 8


def _dot_nt(a, b):
    return lax.dot_general(a, b, (((1,), (1,)), ((), ())), preferred_element_type=F32)


def _dot(a, b):
    return jnp.dot(a, b, preferred_element_type=F32)


ROT128 = HEAD_DIM // ROPE_FRACTION_DIV
ROT64 = IDX_DIM // ROPE_FRACTION_DIV


def _fill_rope_tables(pos_ref, pat_ref, c128_ref, s128_ref, c64_ref, s64_ref):
    pos = pos_ref[...].astype(F32)
    ang = pos * pat_ref[0:1, :]
    cb = jnp.cos(ang)
    sb = jnp.sin(ang) * pat_ref[1:2, :]
    lane = lax.broadcasted_iota(I32, ang.shape, 1)
    wide = lane < ROT128
    c128_ref[...] = jnp.where(wide, cb, 1.0)
    s128_ref[...] = jnp.where(wide, sb, 0.0)
    rotated = (lane % IDX_DIM) < ROT64
    low = lane < IDX_DIM
    c64_ref[...] = jnp.where(rotated, jnp.where(low, pltpu.roll(cb, LANES - ROT128, 1),
                                                pltpu.roll(cb, IDX_DIM - ROT128, 1)), 1.0)
    s64_ref[...] = jnp.where(rotated, jnp.where(low, pltpu.roll(sb, LANES - ROT128, 1),
                                                pltpu.roll(sb, IDX_DIM - ROT128, 1)), 0.0)


def _rope_lane_patterns():
    lane = np.arange(LANES)
    in128 = lane < ROT128
    l64 = lane - ROT128
    in64 = (l64 >= 0) & (l64 < ROT64)
    expo = np.where(in128, -(lane % (ROT128 // 2)) / (ROT128 // 2),
                    np.where(in64, -(l64 % (ROT64 // 2)) / (ROT64 // 2), 0.0)).astype(np.float32)
    sign = np.where(in128, np.where(lane < ROT128 // 2, -1.0, 1.0),
                    np.where(in64, np.where(l64 < ROT64 // 2, -1.0, 1.0), 0.0)).astype(np.float32)
    invf = jnp.where(jnp.asarray(in128 | in64), jnp.power(ROPE_THETA, jnp.asarray(expo)), 0.0)
    return jnp.stack([invf.astype(F32), jnp.asarray(sign)])


def _apply_rope(x, cos, sin, kind):
    lane = lax.broadcasted_iota(I32, x.shape, 1)
    if kind == "rope128":
        half = HEAD_DIM // ROPE_FRACTION_DIV // 2
        first = lane < half
    else:
        half = IDX_DIM // ROPE_FRACTION_DIV // 2
        first = (lane % IDX_DIM) < half
    partner = jnp.where(first, pltpu.roll(x, LANES - half, 1), pltpu.roll(x, half, 1))
    return x * cos + partner * sin


def _proj_kernel(x_ref, g_ref, w_ref, cs_ref, pos_ref, pat_ref, wtail_ref,
                 o_ref, ik_ref, iw_ref, vt_ref, h_scr, c128_ref, s128_ref, c64_ref, s64_ref, *, tile_kinds, iw_scale):
    n = pl.program_id(1)

    def finish(xj, kind):
        if kind == "rope128":
            return _apply_rope(xj, c128_ref[...], s128_ref[...], "rope128")
        if kind == "rope64":
            return _apply_rope(xj, c64_ref[...], s64_ref[...], "rope64")
        if kind == "gate":
            return xj * (1.0 / (1.0 + jnp.exp(-xj)))
        return xj

    @pl.when(n == 0)
    def _():
        _fill_rope_tables(pos_ref, pat_ref, c128_ref, s128_ref, c64_ref, s64_ref)
        xf = x_ref[...]
        ms = jnp.mean(xf * xf, axis=-1, keepdims=True)
        h_scr[...] = (xf * lax.rsqrt(ms + EPS) * g_ref[...]).astype(BF16)
        tail = _dot(h_scr[...], wtail_ref[...])
        for j in range(2):
            ik_ref[:, j * LANES:(j + 1) * LANES] = finish(
                tail[:, j * LANES:(j + 1) * LANES], "rope64").astype(ik_ref.dtype)
        iw_ref[...] = tail[:, 2 * LANES:3 * LANES] * iw_scale

    for pattern, tiles in tile_kinds:
        @pl.when(functools.reduce(jnp.logical_or, [n == t for t in tiles]))
        def _(pattern=pattern):
            acc = _dot(h_scr[...], w_ref[...]) * cs_ref[...]
            for j, kind in enumerate(pattern):
                o_ref[:, j * LANES:(j + 1) * LANES] = finish(
                    acc[:, j * LANES:(j + 1) * LANES], kind).astype(o_ref.dtype)
            heads = [j for j, kind in enumerate(pattern) if kind == "value"]
            if heads:
                cw = vt_ref.shape[2]
                for g, j in enumerate(heads):
                    v_t = acc[:, j * LANES:(j + 1) * LANES].T.astype(vt_ref.dtype)
                    for c in range(vt_ref.shape[0]):
                        vt_ref[c, g * VT_ROWS:g * VT_ROWS + HEAD_DIM, :] = v_t[:, c * cw:(c + 1) * cw]
                        vt_ref[c, g * VT_ROWS + HEAD_DIM:(g + 1) * VT_ROWS, :] = jnp.ones(
                            (VT_ROWS - HEAD_DIM, cw), vt_ref.dtype)


def _proj(x2, g, w, colscale, pos, w_tail, *, n_cols, tn, kinds, iw_scale, t_chunk):
    m, d = x2.shape
    tm = PROJ_ROWS
    per_tile = tn // LANES
    n_tiles = n_cols // tn
    patterns = {}
    for t in range(n_tiles):
        patterns.setdefault(tuple(kinds[t * per_tile:(t + 1) * per_tile]), []).append(t)
    assert sum("value" in p for p in patterns) == 1 and all(len(ts) == 1 for p, ts in patterns.items() if "value" in p)
    t_rows = kinds.count("value") * VT_ROWS
    row_tile = lambda i, j: (i, 0)
    whole = lambda i, j: (0, 0)
    return pl.pallas_call(
        functools.partial(_proj_kernel, tile_kinds=tuple(patterns.items()), iw_scale=iw_scale),
        grid=(m // tm, n_tiles),
        in_specs=[pl.BlockSpec((tm, d), row_tile),
                  pl.BlockSpec((1, d), whole),
                  pl.BlockSpec((d, tn), lambda i, j: (0, j)),
                  pl.BlockSpec((1, tn), lambda i, j: (0, j)),
                  pl.BlockSpec((tm, 1), row_tile),
                  pl.BlockSpec((2, LANES), whole),
                  pl.BlockSpec(w_tail.shape, whole)],
        out_specs=[pl.BlockSpec((tm, tn), lambda i, j: (i, j)),
                   pl.BlockSpec((tm, 2 * LANES), row_tile),
                   pl.BlockSpec((tm, LANES), row_tile),
                   pl.BlockSpec((tm // t_chunk, t_rows, t_chunk), lambda i, j: (i, 0, 0))],
        out_shape=[jax.ShapeDtypeStruct((m, n_cols), BF16),
                   jax.ShapeDtypeStruct((m, 2 * LANES), BF16),
                   jax.ShapeDtypeStruct((m, LANES), F32),
                   jax.ShapeDtypeStruct((m // t_chunk, t_rows, t_chunk), BF16)],
        scratch_shapes=[pltpu.VMEM((tm, d), BF16)] + [pltpu.VMEM((tm, LANES), F32)] * 4,
        compiler_params=pltpu.CompilerParams(
            dimension_semantics=("arbitrary", "arbitrary"), vmem_limit_bytes=VMEM_LIMIT_BYTES),
        name="in_proj",
    )(x2, g, w, colscale, pos, _rope_lane_patterns(), w_tail)


def _unit_rms(o, axis):
    return o * lax.rsqrt(jnp.mean(o * o, axis=axis, keepdims=True) + EPS)


def _gain_gate(y, gain, gate):
    return y * gain * gate.astype(F32)


def _norm_gate(o, gain, gate):
    return _gain_gate(_unit_rms(o, -1), gain, gate)


def _sb_kernel(q_ref, k_ref, v_ref, gate_ref, gain_ref, o_ref, *, tq, hps):
    i = pl.program_id(2)
    row = lax.broadcasted_iota(I32, (tq, tq), 0)
    col = lax.broadcasted_iota(I32, (tq, tq), 1)
    later = jnp.where(row > col, 1.0, 0.0).astype(BF16)
    strict = col < row
    qs = [q_ref[:, h * HEAD_DIM:(h + 1) * HEAD_DIM] for h in range(hps)]

    def block(kb, cs, accs, diag):
        start = pl.multiple_of(kb * tq, tq)
        heads = range(hps)
        zs = [_dot_nt(qs[h], k_ref[pl.ds(start, tq), h * HEAD_DIM:(h + 1) * HEAD_DIM]) for h in heads]
        log_betas, log_1ms, his, los = [], [], [], []
        for h in heads:
            z = zs[h]
            sp = jnp.log(1.0 + jnp.exp(-jnp.abs(z)))
            log_beta = jnp.minimum(z, 0.0) - sp
            log_1m = log_beta - z
            if diag:
                log_beta = jnp.where(strict, log_beta, -jnp.inf)
                log_1m = jnp.where(strict, log_1m, 0.0)
            hi = log_1m.astype(BF16)
            log_betas.append(log_beta)
            log_1ms.append(log_1m)
            his.append(hi)
            los.append((log_1m - hi.astype(F32)).astype(BF16))
        tails = [_dot(his[h], later) + _dot(los[h], later) for h in heads]
        ws = [jnp.exp(log_betas[h] + tails[h] + cs[h]).astype(BF16) for h in heads]
        new_accs = [accs[h] + _dot(ws[h], v_ref[pl.ds(start, tq), h * HEAD_DIM:(h + 1) * HEAD_DIM])
                    for h in heads]
        new_cs = [cs[h] + jnp.sum(log_1ms[h], axis=1, keepdims=True) for h in heads]
        return new_cs, new_accs

    def dead_bound(cs):
        return functools.reduce(jnp.maximum, [jnp.max(c) for c in cs])

    cs0 = [jnp.zeros((tq, 1), F32) for _ in range(hps)]
    accs0 = [jnp.zeros((tq, HEAD_DIM), F32) for _ in range(hps)]
    cs1, accs1 = block(i, cs0, accs0, True)

    def cond(carry):
        kb, bound, _, _ = carry
        return jnp.logical_and(kb >= 0, bound > SB_DEAD_LOG)

    def body(carry):
        kb, _, cs, accs = carry
        cs, accs = block(kb, list(cs), list(accs), False)
        return kb - 1, dead_bound(cs), tuple(cs), tuple(accs)

    _, _, _, accs = lax.while_loop(cond, body, (i - 1, dead_bound(cs1), tuple(cs1), tuple(accs1)))

    for h in range(hps):
        sl = slice(h * HEAD_DIM, (h + 1) * HEAD_DIM)
        o_ref[:, sl] = _norm_gate(accs[h], gain_ref[:, sl], gate_ref[:, sl]).astype(o_ref.dtype)


def _sb_attention(proj, gain, *, batch, seq, q_col, k_col, v_col, gate_col):
    tq, hps = SB_BLOCK, SB_HEADS_PER_STEP
    w = hps * HEAD_DIM
    qb, kb, vb, gb = (c // w for c in (q_col, k_col, v_col, gate_col))
    return pl.pallas_call(
        functools.partial(_sb_kernel, tq=tq, hps=hps),
        grid=(batch, SB_HEADS // hps, seq // tq),
        in_specs=[pl.BlockSpec((None, tq, w), lambda b, h, i: (b, i, qb + h)),
                  pl.BlockSpec((None, seq, w), lambda b, h, i: (b, 0, kb + h)),
                  pl.BlockSpec((None, seq, w), lambda b, h, i: (b, 0, vb + h)),
                  pl.BlockSpec((None, tq, w), lambda b, h, i: (b, i, gb + h)),
                  pl.BlockSpec((1, w), lambda b, h, i: (0, h))],
        out_specs=pl.BlockSpec((None, tq, w), lambda b, h, i: (b, i, h)),
        out_shape=jax.ShapeDtypeStruct((batch, seq, SB_HEADS * HEAD_DIM), BF16),
        compiler_params=pltpu.CompilerParams(
            dimension_semantics=("arbitrary", "arbitrary", "arbitrary"),
            vmem_limit_bytes=VMEM_LIMIT_BYTES),
        name="sb_attention",
    )(proj, proj, proj, proj, gain)


def _dsa_kernel(iq_lo_ref, iq_hi_ref, ik_ref, iw_ref, q_ref, k_ref, vt_ref, gate_lo_ref, gate_hi_ref, gain_ref,
                o_ref, sc_scr, qs_scr, s_scr, p_scr, *, tq, ck, topk, seq):
    i = pl.program_id(1)
    n_chunks = (i * tq) // ck + 1
    rep = DSA_HEADS // DSA_KV_HEADS
    kf = float(topk)
    groups = ck // SUBLANES

    iw_t = iw_ref[...].T
    q_pos = i * tq + lax.broadcasted_iota(I32, (ck, tq), 1)
    key_off = lax.broadcasted_iota(I32, (ck, tq), 0)

    def score_chunk(c, carry):
        mn, mx = carry
        start = pl.multiple_of(c * ck, ck)
        k_even = ik_ref[pl.ds(start, ck), 0:LANES]
        k_odd = ik_ref[pl.ds(start, ck), LANES:2 * LANES]
        acc = jnp.zeros((ck, tq), F32)
        pairs = IDX_HEADS // 2
        for p in range(pairs):
            half_ref, pp = (iq_lo_ref, p) if p < pairs // 2 else (iq_hi_ref, p - pairs // 2)
            qp = half_ref[:, pp * LANES:(pp + 1) * LANES]
            for hh, kk in ((2 * p, k_even), (2 * p + 1, k_odd)):
                acc = acc + iw_t[hh:hh + 1, :] * jnp.maximum(_dot_nt(kk, qp), 0.0)
        causal = (c * ck + key_off) <= q_pos
        sc_scr[c] = jnp.where(causal, acc, -jnp.inf)
        mx = jnp.maximum(mx, jnp.max(jnp.where(causal, acc, -jnp.inf), axis=0, keepdims=True))
        mn = jnp.minimum(mn, jnp.min(jnp.where(causal, acc, jnp.inf), axis=0, keepdims=True))
        return mn, mx

    row_min, row_max = lax.fori_loop(0, n_chunks, score_chunk,
                                     (jnp.full((1, tq), jnp.inf, F32), jnp.full((1, tq), -jnp.inf, F32)))

    def rows8(v):
        return jnp.broadcast_to(v, (SUBLANES, tq))

    def count_where(pred):
        def body(c, cnt):
            hit = jnp.where(pred(sc_scr[c].reshape(groups, SUBLANES, tq), c), 1.0, 0.0)
            return cnt + jnp.sum(hit.reshape(groups // COUNT_LANES, COUNT_LANES, SUBLANES, tq), axis=0)
        cnt = lax.fori_loop(0, n_chunks, body, jnp.zeros((COUNT_LANES, SUBLANES, tq), F32))
        return jnp.sum(jnp.sum(cnt, axis=0), axis=0, keepdims=True)

    def count_ge(mid):
        mid8 = rows8(mid)

        def body(c, cnt):
            cnt = list(cnt)
            for j in range(groups):
                blk = sc_scr[c, j * SUBLANES:(j + 1) * SUBLANES, :]
                cnt[j % COUNT_LANES] = cnt[j % COUNT_LANES] + jnp.where(blk >= mid8, 1.0, 0.0)
            return tuple(cnt)

        cnt = lax.fori_loop(0, n_chunks, body, tuple(jnp.zeros((SUBLANES, tq), F32) for _ in range(COUNT_LANES)))
        return jnp.sum(functools.reduce(lambda a, b: a + b, cnt), axis=0, keepdims=True)

    n_valid = (i * tq + lax.broadcasted_iota(I32, (1, tq), 1) + 1).astype(F32)
    few = n_valid <= kf

    def bisect_pass(state):
        lo, hi, cnt_lo, done = state
        mid = 0.5 * lo + 0.5 * hi
        no_gap = jnp.logical_or(mid <= lo, mid >= hi)
        cnt = count_ge(mid)
        ge = cnt >= kf
        live = done < 0.5
        up = jnp.logical_and(live, ge)
        down = jnp.logical_and(live, jnp.logical_not(ge))
        lo = jnp.where(up, mid, lo)
        cnt_lo = jnp.where(up, cnt, cnt_lo)
        hi = jnp.where(down, mid, hi)
        done = jnp.where(jnp.logical_or(cnt_lo == kf, no_gap), 1.0, done)
        return lo, hi, cnt_lo, done

    def search_cond(carry):
        it, all_done = carry[0], carry[1]
        return jnp.logical_and(it < SEARCH_PASS_CAP, all_done < 0.5)

    def search_body(carry):
        state = carry[2:]
        for _ in range(SEARCH_UNROLL):
            state = bisect_pass(state)
        return (carry[0] + SEARCH_UNROLL, jnp.min(state[3])) + tuple(state)

    hi0 = row_max + (row_max - row_min) * (2.0 ** -10) + 1e-30
    done0 = jnp.where(few, 1.0, 0.0)
    carry = lax.while_loop(search_cond, search_body,
                           (jnp.int32(0), jnp.min(done0), row_min, hi0, n_valid, done0))
    thr, cnt_thr = carry[2], carry[4]
    thr8 = rows8(thr)

    excess = jnp.logical_and(cnt_thr > kf, jnp.logical_not(few))

    @pl.when(jnp.max(jnp.where(excess, 1.0, 0.0)) > 0.5)
    def _():
        need = kf - count_where(lambda s, c: s > thr8)
        key_in_chunk = lax.broadcasted_iota(I32, (groups, SUBLANES, tq), 0) * SUBLANES + lax.broadcasted_iota(
            I32, (groups, SUBLANES, tq), 1)

        def key_f(c):
            return (c * ck + key_in_chunk).astype(F32)

        def pos_pass(b, last_short):
            cand = last_short + jnp.left_shift(jnp.int32(1), b).astype(F32)
            cand8 = rows8(cand)
            cnt = count_where(lambda s, c: jnp.logical_and(s == thr8, key_f(c) <= cand8))
            return jnp.where(cnt < need, cand, last_short)

        n_bits = (seq - 1).bit_length()
        last_short = lax.fori_loop(0, n_bits, lambda t, x: pos_pass(n_bits - 1 - t, x),
                                   jnp.full((1, tq), -1.0, F32))
        keep_upto = rows8(jnp.where(excess, last_short + 1.0, float(seq)))

        def drop(c, carry):
            s = sc_scr[c].reshape(groups, SUBLANES, tq)
            cut = jnp.logical_and(s == thr8, key_f(c) > keep_upto)
            sc_scr[c] = jnp.where(cut, -jnp.inf, s).reshape(ck, tq)
            return carry

        lax.fori_loop(0, n_chunks, drop, 0)

    for g in range(DSA_KV_HEADS):
        for r in range(rep):
            h = g * rep + r
            qs_scr[g, r * tq:(r + 1) * tq, :] = q_ref[:, h * HEAD_DIM:(h + 1) * HEAD_DIM]
    def attend_chunk(c, stats):
        start = pl.multiple_of(c * ck, ck)
        bias = jnp.where(sc_scr[c] >= thr, 0.0, MASK_BIAS)
        bias = jnp.concatenate([bias] * rep, axis=1)
        kv = range(DSA_KV_HEADS)
        m_news = []
        for g in kv:
            kg = k_ref[pl.ds(start, ck), g * HEAD_DIM:(g + 1) * HEAD_DIM]
            s = _dot_nt(kg, qs_scr[g]) + bias
            s_scr[g] = s
            m_news.append(jnp.maximum(stats[g][0], jnp.max(s, axis=0, keepdims=True)))
        out = []
        for g in kv:
            p_scr[g] = jnp.exp2(s_scr[g] - m_news[g]).astype(BF16)
            m_old, l_old, acc_old = stats[g]
            alpha = jnp.exp2(m_old - m_news[g])
            pv = _dot(vt_ref[c, g * VT_ROWS:(g + 1) * VT_ROWS, :], p_scr[g])
            out.append((m_news[g], alpha * l_old + pv[HEAD_DIM:HEAD_DIM + 1],
                        alpha * acc_old + pv[:HEAD_DIM]))
        return tuple(out)

    init = tuple((jnp.full((1, rep * tq), MASK_BIAS, F32), jnp.zeros((1, rep * tq), F32),
                  jnp.zeros((HEAD_DIM, rep * tq), F32)) for _ in range(DSA_KV_HEADS))
    fin = lax.fori_loop(0, n_chunks, attend_chunk, init)

    for g in range(DSA_KV_HEADS):
        o_t = fin[g][2] * (1.0 / fin[g][1])
        y_t = _unit_rms(o_t, 0)
        gate_ref = (gate_lo_ref, gate_hi_ref)[g]
        for r in range(rep):
            sl = slice((g * rep + r) * HEAD_DIM, (g * rep + r + 1) * HEAD_DIM)
            gsl = slice(r * HEAD_DIM, (r + 1) * HEAD_DIM)
            y = y_t[:, r * tq:(r + 1) * tq].T
            o_ref[:, sl] = _gain_gate(y, gain_ref[:, sl], gate_ref[:, gsl]).astype(o_ref.dtype)


def _dsa_attention(proj, ikp, iw, vt, gain, *, batch, seq, iq_col, q_col, k_col, gate_col):
    tq, ck = DSA_QUERIES, DSA_KEYS
    assert seq % ck == 0 and seq % tq == 0
    topk = min(TOPK_MAX, seq // 4)
    dq = DSA_HEADS * HEAD_DIM
    dkv = DSA_KV_HEADS * HEAD_DIM
    rep = DSA_HEADS // DSA_KV_HEADS
    half = dq // 2
    assert IDX_HEADS * IDX_DIM == dq and all(c % half == 0 for c in (iq_col, gate_col))
    assert q_col % dq == 0 and k_col % dkv == 0

    def half_block(col, which):
        return pl.BlockSpec((None, tq, half), lambda b, i: (b, i, col // half + which))

    return pl.pallas_call(
        functools.partial(_dsa_kernel, tq=tq, ck=ck, topk=topk, seq=seq),
        grid=(batch, seq // tq),
        in_specs=[half_block(iq_col, 0), half_block(iq_col, 1),
                  pl.BlockSpec((None, seq, 2 * LANES), lambda b, i: (b, 0, 0)),
                  pl.BlockSpec((None, tq, LANES), lambda b, i: (b, i, 0)),
                  pl.BlockSpec((None, tq, dq), lambda b, i: (b, i, q_col // dq)),
                  pl.BlockSpec((None, seq, dkv), lambda b, i: (b, 0, k_col // dkv)),
                  pl.BlockSpec((None, seq // ck, DSA_KV_HEADS * VT_ROWS, ck), lambda b, i: (b, 0, 0, 0)),
                  half_block(gate_col, 0), half_block(gate_col, 1),
                  pl.BlockSpec((1, dq), lambda b, i: (0, 0))],
        out_specs=pl.BlockSpec((None, tq, dq), lambda b, i: (b, i, 0)),
        out_shape=jax.ShapeDtypeStruct((batch, seq, dq), BF16),
        scratch_shapes=[pltpu.VMEM((seq // ck, ck, tq), F32),
                        pltpu.VMEM((DSA_KV_HEADS, rep * tq, HEAD_DIM), BF16),
                        pltpu.VMEM((DSA_KV_HEADS, ck, rep * tq), F32),
                        pltpu.VMEM((DSA_KV_HEADS, ck, rep * tq), BF16)],
        compiler_params=pltpu.CompilerParams(
            dimension_semantics=("arbitrary", "arbitrary"), vmem_limit_bytes=VMEM_LIMIT_BYTES),
        name="dsa_attention",
    )(proj, proj, ikp, iw, proj, proj, vt, proj, proj, gain)


def _out_kernel(a_ref, b_ref, wa_ref, wb_ref, x_ref, g_ref, o_ref):
    y = x_ref[...] + _dot(a_ref[...], wa_ref[...]) + _dot(b_ref[...], wb_ref[...])
    ms = jnp.mean(y * y, axis=-1, keepdims=True)
    o_ref[...] = y * lax.rsqrt(ms + EPS) * g_ref[...]


def _out_proj(mix_a, mix_b, w, x2, final_g):
    m, d = x2.shape
    tm = OUT_ROWS
    ka, kb = mix_a.shape[1], mix_b.shape[1]
    assert ka == kb and w.shape[0] == ka + kb
    return pl.pallas_call(
        _out_kernel,
        grid=(m // tm,),
        in_specs=[pl.BlockSpec((tm, ka), lambda i: (i, 0)),
                  pl.BlockSpec((tm, kb), lambda i: (i, 0)),
                  pl.BlockSpec((ka, d), lambda i: (0, 0)),
                  pl.BlockSpec((kb, d), lambda i: (1, 0)),
                  pl.BlockSpec((tm, d), lambda i: (i, 0)),
                  pl.BlockSpec((1, d), lambda i: (0, 0))],
        out_specs=pl.BlockSpec((tm, d), lambda i: (i, 0)),
        out_shape=jax.ShapeDtypeStruct((m, d), F32),
        compiler_params=pltpu.CompilerParams(
            dimension_semantics=("arbitrary",), vmem_limit_bytes=VMEM_LIMIT_BYTES),
        name="out_proj",
    )(mix_a, mix_b, w, w, x2, final_g)


def kernel(x, positions, norm_g, w_in, sb_out_g, dsa_out_g, w_out, final_g):
    batch, seq, d_model = x.shape
    d_sb = SB_HEADS * HEAD_DIM
    d_dsa = DSA_HEADS * HEAD_DIM
    d_kv = DSA_KV_HEADS * HEAD_DIM
    d_iq = IDX_HEADS * IDX_DIM
    depth = norm_g.shape[0]
    assert depth == 1 and d_sb + d_dsa == w_out.shape[1]
    m = batch * seq
    x2 = x.reshape(m, d_model)
    layer = 0

    names = ("sb_q", "sb_k", "sb_v", "sb_gate", "dsa_q", "dsa_k", "dsa_v", "dsa_gate", "idx_q", "idx_k", "idx_w")
    sizes = (d_sb, d_sb, d_sb, d_sb, d_dsa, d_kv, d_kv, d_dsa, d_iq, IDX_DIM, IDX_HEADS)
    col, off = {}, 0
    for name, size in zip(names, sizes):
        col[name] = off
        off += size
    n_main = col["idx_k"]
    assert n_main % PROJ_COLS == 0

    w_bf = w_in[layer].astype(BF16)
    w_ik = w_bf[:, col["idx_k"]:col["idx_k"] + IDX_DIM]
    zeros64 = jnp.zeros((d_model, LANES - IDX_DIM), BF16)
    w_tail = jnp.concatenate([w_ik, zeros64, zeros64, w_ik, w_bf[:, col["idx_w"]:col["idx_w"] + IDX_HEADS],
                              jnp.zeros((d_model, LANES - IDX_HEADS), BF16)], axis=1)

    att_scale = HEAD_DIM ** -0.5
    colscale = np.ones((1, n_main), np.float32)
    colscale[:, col["sb_q"]:col["sb_q"] + d_sb] = att_scale
    colscale[:, col["dsa_q"]:col["dsa_q"] + d_dsa] = att_scale * LOG2_E
    kinds = ["plain"] * (n_main // LANES)
    for j in range(col["dsa_q"] // LANES, col["dsa_v"] // LANES):
        kinds[j] = "rope128"
    for j in range(col["dsa_v"] // LANES, col["dsa_gate"] // LANES):
        kinds[j] = "value"
    for name, width in (("sb_gate", d_sb), ("dsa_gate", d_dsa)):
        for j in range(col[name] // LANES, (col[name] + width) // LANES):
            kinds[j] = "gate"
    for j in range(col["idx_q"] // LANES, col["idx_k"] // LANES):
        kinds[j] = "rope64"

    pos = positions.reshape(m, 1).astype(I32)
    proj, ikp, iw, vt = _proj(x2, norm_g[layer].reshape(1, d_model), w_bf, jnp.asarray(colscale), pos, w_tail,
                              n_cols=n_main, tn=PROJ_COLS, kinds=tuple(kinds),
                              iw_scale=(IDX_HEADS ** -0.5) * (IDX_DIM ** -0.5), t_chunk=DSA_KEYS)

    proj3 = proj.reshape(batch, seq, n_main)
    mix_sb = _sb_attention(proj3, sb_out_g[layer].reshape(1, d_sb), batch=batch, seq=seq,
                           q_col=col["sb_q"], k_col=col["sb_k"], v_col=col["sb_v"], gate_col=col["sb_gate"])
    vt4 = vt.reshape(batch, seq // DSA_KEYS, DSA_KV_HEADS * VT_ROWS, DSA_KEYS)
    mix_dsa = _dsa_attention(proj3, ikp.reshape(batch, seq, -1), iw.reshape(batch, seq, -1), vt4,
                             dsa_out_g[layer].reshape(1, d_dsa), batch=batch, seq=seq,
                             iq_col=col["idx_q"], q_col=col["dsa_q"], k_col=col["dsa_k"], gate_col=col["dsa_gate"])

    w_o = w_out[layer].astype(BF16)
    out = _out_proj(mix_sb.reshape(m, d_sb), mix_dsa.reshape(m, d_dsa), w_o, x2, final_g.reshape(1, d_model))
    return out.reshape(batch, seq, d_model)
```
